```python
import jax, jax.numpy as jnp
from jax import lax
import numpy as np

D_MODEL = 1024
BATCH = 8
SEQ = 4096
DEPTH = 4

N_MIXERS = 2
N_LAYERS_A = (DEPTH + 1) // 2
N_LAYERS_B = DEPTH // 2
CHUNK = 128
GMLP_WIDTH = 2 * D_MODEL
GMLP_GROUPS = 8
GMLP_GROUP_DIM = GMLP_WIDTH // GMLP_GROUPS
HEAD_DIM = 64
N_Q_HEADS = D_MODEL // HEAD_DIM
N_KV_HEADS = 4
GQA_GROUP = N_Q_HEADS // N_KV_HEADS
WINDOW = 128
ATTN_BLOCK = 128
ROPE_DIM = HEAD_DIM // 4
ROPE_THETA = 500000.0
Q_WIDTH = N_Q_HEADS * HEAD_DIM
KV_WIDTH = N_KV_HEADS * HEAD_DIM
QKV_WIDTH = Q_WIDTH + 2 * KV_WIDTH
FFN_HIDDEN = -(-(8 * D_MODEL) // (3 * 256)) * 256
RMS_EPS = 1e-6
LN_EPS = 1e-5
NEG_INF = -1e30

kernel_name = "hybrid_gmlp_swa_sink_sandwich"


def rmsnorm(x, g):
    xf = x.astype(jnp.float32)
    y = xf * lax.rsqrt(jnp.mean(xf * xf, axis=-1, keepdims=True) + RMS_EPS)
    return y.astype(x.dtype) * g


def layernorm(x, g, b):
    xf = x.astype(jnp.float32)
    mu = jnp.mean(xf, axis=-1, keepdims=True)
    var = jnp.mean(jnp.square(xf - mu), axis=-1, keepdims=True)
    return ((xf - mu) * lax.rsqrt(var + LN_EPS)).astype(x.dtype) * g + b


def rope_tables(positions):
    inv_freq = ROPE_THETA ** (-jnp.arange(0, ROPE_DIM, 2, dtype=jnp.float32) / ROPE_DIM)
    ang = positions.astype(jnp.float32)[..., None] * inv_freq
    return jnp.cos(ang)[:, :, None, :], jnp.sin(ang)[:, :, None, :]


def apply_partial_rope(x, cos, sin):
    half = ROPE_DIM // 2
    cos = cos.astype(x.dtype)
    sin = sin.astype(x.dtype)
    x1, x2, rest = x[..., :half], x[..., half:ROPE_DIM], x[..., ROPE_DIM:]
    return jnp.concatenate([x1 * cos - x2 * sin, x2 * cos + x1 * sin, rest], axis=-1)


def gmlp_mixer(h, w_in, b_in, ln_g, ln_b, w_s, b_s, w_out):
    B, S, _ = h.shape
    nc = S // CHUNK
    z = jax.nn.gelu(h @ w_in + b_in, approximate=False)
    u, v = jnp.split(z, 2, axis=-1)
    v = layernorm(v, ln_g, ln_b)
    v = v.reshape(B, nc, CHUNK, GMLP_GROUPS, GMLP_GROUP_DIM)
    causal = jnp.tril(jnp.ones((CHUNK, CHUNK), dtype=bool))
    w = jnp.where(causal[None], w_s, 0.0)
    sv = jnp.einsum("gts,bnsgc->bntgc", w, v) + b_s.T[None, None, :, :, None]
    gated = u * sv.reshape(B, S, GMLP_WIDTH)
    return gated @ w_out


def swa_sink_mixer(h, cos, sin, w_qkv, b_qkv, sinks, w_o):
    B, S, _ = h.shape
    nb = S // ATTN_BLOCK
    qkv = h @ w_qkv + b_qkv
    q = qkv[..., :Q_WIDTH].reshape(B, S, N_Q_HEADS, HEAD_DIM)
    k = qkv[..., Q_WIDTH:Q_WIDTH + KV_WIDTH].reshape(B, S, N_KV_HEADS, HEAD_DIM)
    v = qkv[..., Q_WIDTH + KV_WIDTH:].reshape(B, S, N_KV_HEADS, HEAD_DIM)
    q = apply_partial_rope(q, cos, sin) * (HEAD_DIM ** -0.5)
    k = apply_partial_rope(k, cos, sin)
    qb = q.reshape(B, nb, ATTN_BLOCK, N_KV_HEADS, GQA_GROUP, HEAD_DIM)
    kb = k.reshape(B, nb, ATTN_BLOCK, N_KV_HEADS, HEAD_DIM)
    vb = v.reshape(B, nb, ATTN_BLOCK, N_KV_HEADS, HEAD_DIM)
    kk = jnp.concatenate([jnp.concatenate([jnp.zeros_like(kb[:, :1]), kb[:, :-1]], axis=1), kb], axis=2)
    vv = jnp.concatenate([jnp.concatenate([jnp.zeros_like(vb[:, :1]), vb[:, :-1]], axis=1), vb], axis=2)
    s = jnp.einsum("bnqkgd,bnskd->bnkgqs", qb, kk).astype(jnp.float32)
    qi = jnp.arange(ATTN_BLOCK)[:, None]
    sj = jnp.arange(2 * ATTN_BLOCK)[None, :]
    diff = ATTN_BLOCK + qi - sj
    band = (diff >= 0) & (diff < WINDOW)
    exists = (jnp.arange(nb)[:, None, None] > 0) | (sj >= ATTN_BLOCK)[None]
    valid = band[None] & exists
    s = jnp.where(valid[None, :, None, None], s, NEG_INF)
    sink = sinks.astype(jnp.float32).reshape(N_KV_HEADS, GQA_GROUP)[None, None, :, :, None, None]
    m = jnp.maximum(jnp.max(s, axis=-1, keepdims=True), sink)
    p = jnp.exp(s - m)
    denom = jnp.sum(p, axis=-1, keepdims=True) + jnp.exp(sink - m)
    p = (p / denom).astype(vv.dtype)
    o = jnp.einsum("bnkgqs,bnskd->bnqkgd", p, vv).reshape(B, S, Q_WIDTH)
    return o @ w_o


def swiglu_ffn(h, w_gu, w_down):
    g, up = jnp.split(h @ w_gu, 2, axis=-1)
    return (jax.nn.silu(g) * up) @ w_down


def setup_inputs(seed: int = 0) -> dict:
    key = jax.random.key(seed)
    ks = jax.random.split(key, 20)
    f32 = jnp.float32
    nrm = lambda k, shape, scale: jax.random.normal(k, shape, f32) * scale
    x = jax.random.normal(ks[0], (BATCH, SEQ, D_MODEL), f32)
    positions = jnp.broadcast_to(jnp.arange(SEQ, dtype=jnp.int32)[None, :], (BATCH, SEQ))
    gain = lambda k: 1.0 + nrm(k, (DEPTH, D_MODEL), 0.1)
    return {
        "x": x,
        "positions": positions,
        "pre_mix_g": gain(ks[1]),
        "post_mix_g": gain(ks[2]),
        "pre_ffn_g": gain(ks[3]),
        "post_ffn_g": gain(ks[4]),
        "a_w_in": nrm(ks[5], (N_LAYERS_A, D_MODEL, 2 * GMLP_WIDTH), D_MODEL ** -0.5),
        "a_b_in": nrm(ks[6], (N_LAYERS_A, 2 * GMLP_WIDTH), 0.01),
        "a_ln_g": 1.0 + nrm(ks[7], (N_LAYERS_A, GMLP_WIDTH), 0.1),
        "a_ln_b": nrm(ks[8], (N_LAYERS_A, GMLP_WIDTH), 0.01),
        "a_w_s": nrm(ks[9], (N_LAYERS_A, GMLP_GROUPS, CHUNK, CHUNK), 0.5 * CHUNK ** -0.5),
        "a_b_s": 1.0 + nrm(ks[10], (N_LAYERS_A, GMLP_GROUPS, CHUNK), 0.1),
        "a_w_out": nrm(ks[11], (N_LAYERS_A, GMLP_WIDTH, D_MODEL), GMLP_WIDTH ** -0.5),
        "b_w_qkv": nrm(ks[12], (N_LAYERS_B, D_MODEL, QKV_WIDTH), D_MODEL ** -0.5),
        "b_b_qkv": nrm(ks[13], (N_LAYERS_B, QKV_WIDTH), 0.01),
        "b_sinks": nrm(ks[14], (N_LAYERS_B, N_Q_HEADS), 1.0),
        "b_w_o": nrm(ks[15], (N_LAYERS_B, Q_WIDTH, D_MODEL), Q_WIDTH ** -0.5),
        "ffn_w_gu": nrm(ks[16], (DEPTH, D_MODEL, 2 * FFN_HIDDEN), D_MODEL ** -0.5),
        "ffn_w_down": nrm(ks[17], (DEPTH, FFN_HIDDEN, D_MODEL), FFN_HIDDEN ** -0.5),
    }


def reference(x, positions, pre_mix_g, post_mix_g, pre_ffn_g, post_ffn_g,
              a_w_in, a_b_in, a_ln_g, a_ln_b, a_w_s, a_b_s, a_w_out,
              b_w_qkv, b_b_qkv, b_sinks, b_w_o, ffn_w_gu, ffn_w_down):
    cos, sin = rope_tables(positions)
    h = x
    for i in range(DEPTH):
        j = i // N_MIXERS
        hn = rmsnorm(h, pre_mix_g[i])
        if i % N_MIXERS == 0:
            mix = gmlp_mixer(hn, a_w_in[j], a_b_in[j], a_ln_g[j], a_ln_b[j],
                             a_w_s[j], a_b_s[j], a_w_out[j])
        else:
            mix = swa_sink_mixer(hn, cos, sin, b_w_qkv[j], b_b_qkv[j], b_sinks[j], b_w_o[j])
        h = h + rmsnorm(mix, post_mix_g[i])
        f = swiglu_ffn(rmsnorm(h, pre_ffn_g[i]), ffn_w_gu[i], ffn_w_down[i])
        h = h + rmsnorm(f, post_ffn_g[i])
    return h
```

```python
import functools

import jax
import jax.numpy as jnp
from jax import lax
from jax.experimental import pallas as pl
from jax.experimental.pallas import tpu as pltpu

D_MODEL = 1024
CHUNK = 128
GMLP_WIDTH = 2 * D_MODEL
GMLP_GROUPS = 8
GMLP_GROUP_DIM = GMLP_WIDTH // GMLP_GROUPS
HEAD_DIM = 64
N_Q_HEADS = D_MODEL // HEAD_DIM
N_KV_HEADS = 4
GQA_GROUP = N_Q_HEADS // N_KV_HEADS
WINDOW = 128
ATTN_BLOCK = 128
ROPE_DIM = HEAD_DIM // 4
ROPE_THETA = 500000.0
Q_WIDTH = N_Q_HEADS * HEAD_DIM
KV_WIDTH = N_KV_HEADS * HEAD_DIM
FFN_HIDDEN = -(-(8 * D_MODEL) // (3 * 256)) * 256
RMS_EPS = 1e-6
LN_EPS = 1e-5
NEG_INF = -1e30

V7X_LANES = 128
V7X_VMEM_BYTES = 64 * 1024 * 1024

F32 = jnp.float32
BF16 = jnp.bfloat16

TOKEN_TILE = 512
GMLP_IN_COLS = 512
FFN_HIDDEN_COLS = 256


def _vmem_limit(buffer_bytes):
    return int(min(2 * buffer_bytes, V7X_VMEM_BYTES - 8 * 1024 * 1024))


def _rmsnorm(x, g):
    ms = jnp.mean(x * x, axis=-1, keepdims=True)
    return x * lax.rsqrt(ms + RMS_EPS) * g


def _resident(shape):
    zeros = (0,) * len(shape)
    return pl.BlockSpec(shape, lambda *_: zeros)


def _ffn_body(h_ref, pre_ref, post_ref, wgu_ref, wd_ref, o_ref, xn_ref, act_ref):
    h = h_ref[...]
    xn_ref[...] = _rmsnorm(h, pre_ref[...]).astype(BF16)
    for c in range(FFN_HIDDEN // FFN_HIDDEN_COLS):
        lo = c * FFN_HIDDEN_COLS
        gate = jnp.dot(xn_ref[...], wgu_ref[:, lo:lo + FFN_HIDDEN_COLS],
                       preferred_element_type=F32)
        up = jnp.dot(xn_ref[...],
                     wgu_ref[:, FFN_HIDDEN + lo:FFN_HIDDEN + lo + FFN_HIDDEN_COLS],
                     preferred_element_type=F32)
        act = gate * (1.0 / (1.0 + jnp.exp(-gate))) * up
        act_ref[:, lo:lo + FFN_HIDDEN_COLS] = act.astype(BF16)
    f = jnp.dot(act_ref[...], wd_ref[...], preferred_element_type=F32)
    o_ref[...] = h + _rmsnorm(f, post_ref[...])


def _ffn_layer(h, pre_g, post_g, w_gu, w_down):
    tokens = h.shape[0]
    tm = TOKEN_TILE
    buffers = (4 * tm * D_MODEL * 4 + w_gu.size * 2 + w_down.size * 2
               + tm * D_MODEL * 2 + tm * FFN_HIDDEN * 2)
    return pl.pallas_call(
        _ffn_body,
        out_shape=jax.ShapeDtypeStruct((tokens, D_MODEL), F32),
        grid=(tokens // tm,),
        in_specs=[
            pl.BlockSpec((tm, D_MODEL), lambda i: (i, 0)),
            _resident((1, D_MODEL)),
            _resident((1, D_MODEL)),
            _resident(w_gu.shape),
            _resident(w_down.shape),
        ],
        out_specs=pl.BlockSpec((tm, D_MODEL), lambda i: (i, 0)),
        scratch_shapes=[
            pltpu.VMEM((tm, D_MODEL), BF16),
            pltpu.VMEM((tm, FFN_HIDDEN), BF16),
        ],
        compiler_params=pltpu.CompilerParams(
            dimension_semantics=("arbitrary",),
            vmem_limit_bytes=_vmem_limit(buffers)),
        name="ffn",
    )(h, pre_g, post_g, w_gu, w_down)


def _gmlp_body(h_ref, pre_ref, post_ref, win_ref, bin_ref, lng_ref, lnb_ref,
               ws_ref, bst_ref, wout_ref, o_ref,
               xn_ref, u_ref, v_ref, gated_ref):
    tm = h_ref.shape[0]
    h = h_ref[...]
    xn_ref[...] = _rmsnorm(h, pre_ref[...]).astype(BF16)

    for c in range(2 * GMLP_WIDTH // GMLP_IN_COLS):
        lo = c * GMLP_IN_COLS
        z = jnp.dot(xn_ref[...], win_ref[:, lo:lo + GMLP_IN_COLS],
                    preferred_element_type=F32) + bin_ref[:, lo:lo + GMLP_IN_COLS]
        z = 0.5 * z * (1.0 + lax.erf(z * 0.7071067811865476))
        if lo < GMLP_WIDTH:
            u_ref[:, lo:lo + GMLP_IN_COLS] = z
        else:
            v_ref[:, lo - GMLP_WIDTH:lo - GMLP_WIDTH + GMLP_IN_COLS] = z

    row = lax.broadcasted_iota(jnp.int32, (CHUNK, CHUNK), 0)
    col = lax.broadcasted_iota(jnp.int32, (CHUNK, CHUNK), 1)
    causal = col <= row
    w_spatial = [jnp.where(causal, ws_ref[g], 0.0).astype(BF16)
                 for g in range(GMLP_GROUPS)]

    for r in range(tm // CHUNK):
        rows = slice(r * CHUNK, (r + 1) * CHUNK)
        v = v_ref[rows, :]
        mu = jnp.mean(v, axis=-1, keepdims=True)
        d = v - mu
        var = jnp.mean(d * d, axis=-1, keepdims=True)
        vn = (d * lax.rsqrt(var + LN_EPS) * lng_ref[...] + lnb_ref[...]).astype(BF16)
        for g in range(GMLP_GROUPS):
            cols = slice(g * GMLP_GROUP_DIM, (g + 1) * GMLP_GROUP_DIM)
            sv = jnp.dot(w_spatial[g], vn[:, cols], preferred_element_type=F32)
            sv = sv + bst_ref[:, g:g + 1]
            gated_ref[rows, cols] = (u_ref[rows, cols] * sv).astype(BF16)

    mix = jnp.dot(gated_ref[...], wout_ref[...], preferred_element_type=F32)
    o_ref[...] = h + _rmsnorm(mix, post_ref[...])


def _gmlp_layer(h, pre_g, post_g, w_in, b_in, ln_g, ln_b, w_s, b_s_t, w_out):
    tokens = h.shape[0]
    tm = TOKEN_TILE
    buffers = (4 * tm * D_MODEL * 4 + w_in.size * 2 + w_out.size * 2
               + tm * D_MODEL * 2 + 2 * tm * GMLP_WIDTH * 4 + tm * GMLP_WIDTH * 2)
    return pl.pallas_call(
        _gmlp_body,
        out_shape=jax.ShapeDtypeStruct((tokens, D_MODEL), F32),
        grid=(tokens // tm,),
        in_specs=[
            pl.BlockSpec((tm, D_MODEL), lambda i: (i, 0)),
            _resident((1, D_MODEL)),
            _resident((1, D_MODEL)),
            _resident(w_in.shape),
            _resident(b_in.shape),
            _resident(ln_g.shape),
            _resident(ln_b.shape),
            _resident(w_s.shape),
            _resident(b_s_t.shape),
            _resident(w_out.shape),
        ],
        out_specs=pl.BlockSpec((tm, D_MODEL), lambda i: (i, 0)),
        scratch_shapes=[
            pltpu.VMEM((tm, D_MODEL), BF16),
            pltpu.VMEM((tm, GMLP_WIDTH), F32),
            pltpu.VMEM((tm, GMLP_WIDTH), F32),
            pltpu.VMEM((tm, GMLP_WIDTH), BF16),
        ],
        compiler_params=pltpu.CompilerParams(
            dimension_semantics=("arbitrary",),
            vmem_limit_bytes=_vmem_limit(buffers)),
        name="gmlp",
    )(h, pre_g, post_g, w_in, b_in, ln_g, ln_b, w_s, b_s_t, w_out)


KV_DUP_WIDTH = 2 * KV_WIDTH


def _rope(x, cos_tab, sin_tab, is_low):
    partner = jnp.where(is_low,
                        pltpu.roll(x, V7X_LANES - ROPE_DIM // 2, axis=1),
                        pltpu.roll(x, ROPE_DIM // 2, axis=1))
    return x * cos_tab + partner * sin_tab


def _attn_body(sinks_ref, h_ref, cos_ref, sin_ref, pre_ref, post_ref,
               wqkv_ref, bqkv_ref, wo_ref, o_ref,
               xn_ref, q_ref, kd_ref, vd_ref, att_ref):
    tm = h_ref.shape[0]
    seq_tile = pl.program_id(1)
    h = h_ref[...]
    xn_ref[...] = _rmsnorm(h, pre_ref[...]).astype(BF16)

    @pl.when(seq_tile == 0)
    def _():
        kd_ref[0:ATTN_BLOCK, :] = jnp.zeros((ATTN_BLOCK, KV_DUP_WIDTH), BF16)
        vd_ref[0:ATTN_BLOCK, :] = jnp.zeros((ATTN_BLOCK, KV_DUP_WIDTH), BF16)

    lane = lax.broadcasted_iota(jnp.int32, (tm, V7X_LANES), 1)
    first_head = lane < HEAD_DIM
    is_low = (lane % HEAD_DIM) < (ROPE_DIM // 2)
    cos_tab = cos_ref[...]
    sin_tab = sin_ref[...]

    for j in range(Q_WIDTH // V7X_LANES):
        cols = slice(j * V7X_LANES, (j + 1) * V7X_LANES)
        qj = jnp.dot(xn_ref[...], wqkv_ref[:, cols],
                     preferred_element_type=F32) + bqkv_ref[:, cols]
        qj = _rope(qj, cos_tab, sin_tab, is_low) * (HEAD_DIM ** -0.5)
        q_ref[:, cols] = qj.astype(BF16)

    for j in range(KV_WIDTH // V7X_LANES):
        kcols = slice(Q_WIDTH + j * V7X_LANES, Q_WIDTH + (j + 1) * V7X_LANES)
        vcols = slice(Q_WIDTH + KV_WIDTH + j * V7X_LANES,
                      Q_WIDTH + KV_WIDTH + (j + 1) * V7X_LANES)
        kj = jnp.dot(xn_ref[...], wqkv_ref[:, kcols],
                     preferred_element_type=F32) + bqkv_ref[:, kcols]
        kj = _rope(kj, cos_tab, sin_tab, is_low)
        vj = jnp.dot(xn_ref[...], wqkv_ref[:, vcols],
                     preferred_element_type=F32) + bqkv_ref[:, vcols]
        for x, dst in ((kj, kd_ref), (vj, vd_ref)):
            swapped = pltpu.roll(x, HEAD_DIM, axis=1)
            even = jnp.where(first_head, x, swapped)
            odd = jnp.where(first_head, swapped, x)
            dst[ATTN_BLOCK:ATTN_BLOCK + tm,
                (2 * j) * V7X_LANES:(2 * j + 1) * V7X_LANES] = even.astype(BF16)
            dst[ATTN_BLOCK:ATTN_BLOCK + tm,
                (2 * j + 1) * V7X_LANES:(2 * j + 2) * V7X_LANES] = odd.astype(BF16)

    rows4 = GQA_GROUP * ATTN_BLOCK
    qi = lax.broadcasted_iota(jnp.int32, (rows4, 2 * ATTN_BLOCK), 0) % ATTN_BLOCK
    sj = lax.broadcasted_iota(jnp.int32, (rows4, 2 * ATTN_BLOCK), 1)
    diff = ATTN_BLOCK + qi - sj
    band = (diff >= 0) & (diff < WINDOW)
    first_key = jnp.where(seq_tile == 0, ATTN_BLOCK, 0)
    band_first = band & (sj >= first_key)
    half = lax.broadcasted_iota(jnp.int32, (ATTN_BLOCK, V7X_LANES), 1) < HEAD_DIM
    zero_q = jnp.zeros((ATTN_BLOCK, V7X_LANES), BF16)

    for n in range(tm // ATTN_BLOCK):
        rows = slice(n * ATTN_BLOCK, (n + 1) * ATTN_BLOCK)
        keys = slice(n * ATTN_BLOCK, (n + 2) * ATTN_BLOCK)
        valid = band_first if n == 0 else band
        for kv in range(N_KV_HEADS):
            kvcols = slice(kv * V7X_LANES, (kv + 1) * V7X_LANES)
            stacked = []
            sink_rows = []
            for pair in range(GQA_GROUP // 2):
                qcols = slice((2 * kv + pair) * V7X_LANES,
                              (2 * kv + pair + 1) * V7X_LANES)
                qp = q_ref[rows, qcols]
                stacked.append(jnp.where(half, qp, zero_q))
                stacked.append(jnp.where(half, zero_q, qp))
                for e in range(2):
                    head = GQA_GROUP * kv + 2 * pair + e
                    sink_rows.append(jnp.full((ATTN_BLOCK, 1), sinks_ref[head], F32))
            q4 = jnp.concatenate(stacked, axis=0)
            sink = jnp.concatenate(sink_rows, axis=0)
            s = lax.dot_general(q4, kd_ref[keys, kvcols],
                                (((1,), (1,)), ((), ())),
                                preferred_element_type=F32)
            s = jnp.where(valid, s, NEG_INF)
            m = jnp.maximum(jnp.max(s, axis=-1, keepdims=True), sink)
            p = jnp.exp(s - m)
            denom = jnp.sum(p, axis=-1, keepdims=True) + jnp.exp(sink - m)
            o4 = jnp.dot(p.astype(BF16), vd_ref[keys, kvcols],
                         preferred_element_type=F32)
            o4 = o4 * (1.0 / denom)
            for pair in range(GQA_GROUP // 2):
                qcols = slice((2 * kv + pair) * V7X_LANES,
                              (2 * kv + pair + 1) * V7X_LANES)
                lo = o4[(2 * pair) * ATTN_BLOCK:(2 * pair + 1) * ATTN_BLOCK]
                hi = o4[(2 * pair + 1) * ATTN_BLOCK:(2 * pair + 2) * ATTN_BLOCK]
                att_ref[rows, qcols] = jnp.where(half, lo, hi).astype(BF16)

    kd_ref[0:ATTN_BLOCK, :] = kd_ref[tm:tm + ATTN_BLOCK, :]
    vd_ref[0:ATTN_BLOCK, :] = vd_ref[tm:tm + ATTN_BLOCK, :]

    mix = jnp.dot(att_ref[...], wo_ref[...], preferred_element_type=F32)
    o_ref[...] = h + _rmsnorm(mix, post_ref[...])


def _attn_layer(h, cos_tab, sin_tab, pre_g, post_g, w_qkv, b_qkv, sinks, w_o,
                batch, seq):
    tokens = h.shape[0]
    tm = TOKEN_TILE
    tiles = seq // tm
    buffers = (4 * tm * D_MODEL * 4 + 4 * tm * V7X_LANES * 4
               + w_qkv.size * 2 + w_o.size * 2 + 3 * tm * D_MODEL * 2
               + 2 * (tm + ATTN_BLOCK) * KV_DUP_WIDTH * 2)
    tile_map = lambda b, i: (b * tiles + i, 0)
    return pl.pallas_call(
        _attn_body,
        out_shape=jax.ShapeDtypeStruct((tokens, D_MODEL), F32),
        grid=(batch, tiles),
        in_specs=[
            pl.BlockSpec(memory_space=pltpu.SMEM),
            pl.BlockSpec((tm, D_MODEL), tile_map),
            pl.BlockSpec((tm, V7X_LANES), tile_map),
            pl.BlockSpec((tm, V7X_LANES), tile_map),
            _resident((1, D_MODEL)),
            _resident((1, D_MODEL)),
            _resident(w_qkv.shape),
            _resident(b_qkv.shape),
            _resident(w_o.shape),
        ],
        out_specs=pl.BlockSpec((tm, D_MODEL), tile_map),
        scratch_shapes=[
            pltpu.VMEM((tm, D_MODEL), BF16),
            pltpu.VMEM((tm, Q_WIDTH), BF16),
            pltpu.VMEM((tm + ATTN_BLOCK, KV_DUP_WIDTH), BF16),
            pltpu.VMEM((tm + ATTN_BLOCK, KV_DUP_WIDTH), BF16),
            pltpu.VMEM((tm, Q_WIDTH), BF16),
        ],
        compiler_params=pltpu.CompilerParams(
            dimension_semantics=("arbitrary", "arbitrary"),
            vmem_limit_bytes=_vmem_limit(buffers)),
        name="swa",
    )(sinks, h, cos_tab, sin_tab, pre_g, post_g, w_qkv, b_qkv, w_o)


def _rope_lane_tables(positions):
    half = ROPE_DIM // 2
    inv_freq = ROPE_THETA ** (-jnp.arange(0, ROPE_DIM, 2, dtype=F32) / ROPE_DIM)
    ang = positions.astype(F32).reshape(-1, 1) * inv_freq
    cos, sin = jnp.cos(ang), jnp.sin(ang)
    tokens = ang.shape[0]
    rest = HEAD_DIM - ROPE_DIM
    cos_head = jnp.concatenate([cos, cos, jnp.ones((tokens, rest), F32)], axis=1)
    sin_head = jnp.concatenate([-sin, sin, jnp.zeros((tokens, rest), F32)], axis=1)
    reps = V7X_LANES // HEAD_DIM
    return jnp.tile(cos_head, (1, reps)), jnp.tile(sin_head, (1, reps))


def kernel(x, positions, pre_mix_g, post_mix_g, pre_ffn_g, post_ffn_g,
           a_w_in, a_b_in, a_ln_g, a_ln_b, a_w_s, a_b_s, a_w_out,
           b_w_qkv, b_b_qkv, b_sinks, b_w_o, ffn_w_gu, ffn_w_down):
    batch, seq, d_model = x.shape
    depth = pre_mix_g.shape[0]
    assert d_model == D_MODEL and seq % TOKEN_TILE == 0
    cos_tab, sin_tab = _rope_lane_tables(positions)
    row = lambda v: v.reshape(1, -1)
    h = x.reshape(batch * seq, d_model)
    for i in range(depth):
        j = i // 2
        if i % 2 == 0:
            h = _gmlp_layer(h, row(pre_mix_g[i]), row(post_mix_g[i]),
                            a_w_in[j].astype(BF16), row(a_b_in[j]),
                            row(a_ln_g[j]), row(a_ln_b[j]),
                            a_w_s[j], a_b_s[j].T, a_w_out[j].astype(BF16))
        else:
            h = _attn_layer(h, cos_tab, sin_tab, row(pre_mix_g[i]), row(post_mix_g[i]),
                            b_w_qkv[j].astype(BF16), row(b_b_qkv[j]), b_sinks[j],
                            b_w_o[j].astype(BF16), batch, seq)
        h = _ffn_layer(h, row(pre_ffn_g[i]), row(post_ffn_g[i]),
                       ffn_w_gu[i].astype(BF16), ffn_w_down[i].astype(BF16))
    return h.reshape(batch, seq, d_model)
```

```python
import functools

import jax
import jax.numpy as jnp
from jax import lax
from jax.experimental import pallas as pl
from jax.experimental.pallas import tpu as pltpu

D_MODEL = 1024
CHUNK = 128
GMLP_WIDTH = 2 * D_MODEL
GMLP_GROUPS = 8
GMLP_GROUP_DIM = GMLP_WIDTH // GMLP_GROUPS
HEAD_DIM = 64
N_Q_HEADS = D_MODEL // HEAD_DIM
N_KV_HEADS = 4
GQA_GROUP = N_Q_HEADS // N_KV_HEADS
WINDOW = 128
ATTN_BLOCK = 128
ROPE_DIM = HEAD_DIM // 4
ROPE_THETA = 500000.0
Q_WIDTH = N_Q_HEADS * HEAD_DIM
KV_WIDTH = N_KV_HEADS * HEAD_DIM
FFN_HIDDEN = -(-(8 * D_MODEL) // (3 * 256)) * 256
RMS_EPS = 1e-6
LN_EPS = 1e-5
NEG_INF = -1e30

V7X_LANES = 128
V7X_VMEM_BYTES = 64 * 1024 * 1024

F32 = jnp.float32
BF16 = jnp.bfloat16

FFN_TILE = 1024
GMLP_TILE = 1024
ATTN_TILE = 512
ROW_SLAB = 256
RING_SLABS = 2
GMLP_IN_COLS = 512
FFN_HIDDEN_COLS = 256


def _vmem_limit(buffer_bytes):
    return int(min(2 * buffer_bytes, V7X_VMEM_BYTES - 8 * 1024 * 1024))


def _rmsnorm(x, g):
    ms = jnp.mean(x * x, axis=-1, keepdims=True)
    return x * lax.rsqrt(ms + RMS_EPS) * g


def _resident(shape):
    zeros = (0,) * len(shape)
    return pl.BlockSpec(shape, lambda *_: zeros)


def _ffn_body(h_ref, pre_ref, post_ref, wgu_ref, wd_ref, o_ref, xn_ref, act_ref):
    n_slabs = h_ref.shape[0] // ROW_SLAB

    def rows(s):
        return slice(s * ROW_SLAB, (s + 1) * ROW_SLAB)

    def prologue(s):
        xn_ref[rows(s), :] = _rmsnorm(h_ref[rows(s), :], pre_ref[...]).astype(BF16)

    def gate_up(s):
        xn = xn_ref[rows(s), :]
        for c in range(FFN_HIDDEN // FFN_HIDDEN_COLS):
            lo = c * FFN_HIDDEN_COLS
            gate = jnp.dot(xn, wgu_ref[:, lo:lo + FFN_HIDDEN_COLS],
                           preferred_element_type=F32)
            up = jnp.dot(xn, wgu_ref[:, FFN_HIDDEN + lo:FFN_HIDDEN + lo + FFN_HIDDEN_COLS],
                         preferred_element_type=F32)
            act = gate * (1.0 / (1.0 + jnp.exp(-gate))) * up
            act_ref[rows(s), lo:lo + FFN_HIDDEN_COLS] = act.astype(BF16)

    def down(s):
        f = jnp.dot(act_ref[rows(s), :], wd_ref[...], preferred_element_type=F32)
        o_ref[rows(s), :] = h_ref[rows(s), :] + _rmsnorm(f, post_ref[...])

    prologue(0)
    for s in range(n_slabs):
        if s + 1 < n_slabs:
            prologue(s + 1)
        gate_up(s)
        if s > 0:
            down(s - 1)
    down(n_slabs - 1)


def _ffn_layer(h, pre_g, post_g, w_gu, w_down):
    tokens = h.shape[0]
    tm = FFN_TILE
    buffers = (4 * tm * D_MODEL * 4 + w_gu.size * 2 + w_down.size * 2
               + tm * D_MODEL * 2 + tm * FFN_HIDDEN * 2)
    return pl.pallas_call(
        _ffn_body,
        out_shape=jax.ShapeDtypeStruct((tokens, D_MODEL), F32),
        grid=(tokens // tm,),
        in_specs=[
            pl.BlockSpec((tm, D_MODEL), lambda i: (i, 0)),
            _resident((1, D_MODEL)),
            _resident((1, D_MODEL)),
            _resident(w_gu.shape),
            _resident(w_down.shape),
        ],
        out_specs=pl.BlockSpec((tm, D_MODEL), lambda i: (i, 0)),
        scratch_shapes=[
            pltpu.VMEM((tm, D_MODEL), BF16),
            pltpu.VMEM((tm, FFN_HIDDEN), BF16),
        ],
        compiler_params=pltpu.CompilerParams(
            dimension_semantics=("arbitrary",),
            vmem_limit_bytes=_vmem_limit(buffers)),
        name="ffn",
    )(h, pre_g, post_g, w_gu, w_down)


def _gmlp_body(h_ref, pre_ref, post_ref, win_ref, bin_ref, lng_ref, lnb_ref,
               ws_ref, bst_ref, wout_ref, o_ref,
               xn_ref, u_ref, v_ref, gated_ref):
    n_slabs = h_ref.shape[0] // ROW_SLAB

    def rows(s):
        return slice(s * ROW_SLAB, (s + 1) * ROW_SLAB)

    def ring(s):
        return rows(s % RING_SLABS)

    def prologue(s):
        xn_ref[rows(s), :] = _rmsnorm(h_ref[rows(s), :], pre_ref[...]).astype(BF16)

    def project_in(s):
        xn = xn_ref[rows(s), :]
        for c in range(2 * GMLP_WIDTH // GMLP_IN_COLS):
            lo = c * GMLP_IN_COLS
            z = jnp.dot(xn, win_ref[:, lo:lo + GMLP_IN_COLS],
                        preferred_element_type=F32) + bin_ref[:, lo:lo + GMLP_IN_COLS]
            z2 = z * (1.0 + lax.erf(z * 0.7071067811865476))
            if lo < GMLP_WIDTH:
                u_ref[ring(s), lo:lo + GMLP_IN_COLS] = z2
            else:
                v_ref[ring(s), lo - GMLP_WIDTH:lo - GMLP_WIDTH + GMLP_IN_COLS] = z2

    row = lax.broadcasted_iota(jnp.int32, (CHUNK, CHUNK), 0)
    col = lax.broadcasted_iota(jnp.int32, (CHUNK, CHUNK), 1)
    causal = col <= row
    half_w_spatial = [jnp.where(causal, 0.5 * ws_ref[g], 0.0).astype(BF16)
                      for g in range(GMLP_GROUPS)]
    half_b_spatial = 0.5 * bst_ref[...]

    def spatial_gate(s):
        base = (s % RING_SLABS) * ROW_SLAB
        for r in range(ROW_SLAB // CHUNK):
            chunk = slice(base + r * CHUNK, base + (r + 1) * CHUNK)
            v2 = v_ref[chunk, :]
            mu = jnp.mean(v2, axis=-1, keepdims=True)
            d = v2 - mu
            var = jnp.mean(d * d, axis=-1, keepdims=True)
            vn = (d * lax.rsqrt(var + 4.0 * LN_EPS) * lng_ref[...] + lnb_ref[...]).astype(BF16)
            for g in range(GMLP_GROUPS):
                cols = slice(g * GMLP_GROUP_DIM, (g + 1) * GMLP_GROUP_DIM)
                half_sv = jnp.dot(half_w_spatial[g], vn[:, cols], preferred_element_type=F32)
                half_sv = half_sv + half_b_spatial[:, g:g + 1]
                gated_ref[chunk, cols] = (u_ref[chunk, cols] * half_sv).astype(BF16)

    def project_out(s):
        mix = jnp.dot(gated_ref[ring(s), :], wout_ref[...], preferred_element_type=F32)
        o_ref[rows(s), :] = h_ref[rows(s), :] + _rmsnorm(mix, post_ref[...])

    prologue(0)
    project_in(0)
    for s in range(n_slabs):
        if s + 1 < n_slabs:
            prologue(s + 1)
            project_in(s + 1)
        spatial_gate(s)
        project_out(s)


def _gmlp_layer(h, pre_g, post_g, w_in, b_in, ln_g, ln_b, w_s, b_s_t, w_out):
    tokens = h.shape[0]
    tm = GMLP_TILE
    ring_rows = RING_SLABS * ROW_SLAB
    buffers = (4 * tm * D_MODEL * 4 + w_in.size * 2 + w_out.size * 2
               + tm * D_MODEL * 2 + 2 * ring_rows * GMLP_WIDTH * 4
               + ring_rows * GMLP_WIDTH * 2)
    return pl.pallas_call(
        _gmlp_body,
        out_shape=jax.ShapeDtypeStruct((tokens, D_MODEL), F32),
        grid=(tokens // tm,),
        in_specs=[
            pl.BlockSpec((tm, D_MODEL), lambda i: (i, 0)),
            _resident((1, D_MODEL)),
            _resident((1, D_MODEL)),
            _resident(w_in.shape),
            _resident(b_in.shape),
            _resident(ln_g.shape),
            _resident(ln_b.shape),
            _resident(w_s.shape),
            _resident(b_s_t.shape),
            _resident(w_out.shape),
        ],
        out_specs=pl.BlockSpec((tm, D_MODEL), lambda i: (i, 0)),
        scratch_shapes=[
            pltpu.VMEM((tm, D_MODEL), BF16),
            pltpu.VMEM((ring_rows, GMLP_WIDTH), F32),
            pltpu.VMEM((ring_rows, GMLP_WIDTH), F32),
            pltpu.VMEM((ring_rows, GMLP_WIDTH), BF16),
        ],
        compiler_params=pltpu.CompilerParams(
            dimension_semantics=("arbitrary",),
            vmem_limit_bytes=_vmem_limit(buffers)),
        name="gmlp",
    )(h, pre_g, post_g, w_in, b_in, ln_g, ln_b, w_s, b_s_t, w_out)


KV_DUP_WIDTH = 2 * KV_WIDTH
QKV_COLS = 512
Q_SUB = 64
SUB_KEYS = Q_SUB + WINDOW
SUM_ROWS = 16
LOG2_E = 1.4426950408889634
SCORE_LOOKAHEAD = 4


def _rope(x, cos_tab, sin_tab, is_low):
    partner = jnp.where(is_low,
                        pltpu.roll(x, V7X_LANES - ROPE_DIM // 2, axis=1),
                        pltpu.roll(x, ROPE_DIM // 2, axis=1))
    return x * cos_tab + partner * sin_tab


def _attn_body(sinks_ref, h_ref, cos_ref, sin_ref, pre_ref, post_ref,
               wqkv_ref, bqkv_ref, wo_ref, o_ref,
               xn_ref, q_ref, kd_ref, vt_ref, ot_ref):
    tm = h_ref.shape[0]
    seq_tile = pl.program_id(1)
    h = h_ref[...]
    xn_ref[...] = _rmsnorm(h, pre_ref[...]).astype(BF16)

    @pl.when(seq_tile == 0)
    def _():
        kd_ref[0:ATTN_BLOCK, :] = jnp.zeros((ATTN_BLOCK, KV_DUP_WIDTH), BF16)
        vt_ref[:, 0:ATTN_BLOCK] = jnp.zeros((KV_WIDTH, ATTN_BLOCK), BF16)

    lane = lax.broadcasted_iota(jnp.int32, (tm, V7X_LANES), 1)
    first_head = lane < HEAD_DIM
    is_low = (lane % HEAD_DIM) < (ROPE_DIM // 2)
    cos_k = cos_ref[...]
    sin_k = sin_ref[...]
    cos_q = cos_k * (HEAD_DIM ** -0.5 * LOG2_E)
    sin_q = sin_k * (HEAD_DIM ** -0.5 * LOG2_E)

    for c in range(Q_WIDTH // QKV_COLS):
        lo = c * QKV_COLS
        qc = jnp.dot(xn_ref[...], wqkv_ref[:, lo:lo + QKV_COLS],
                     preferred_element_type=F32) + bqkv_ref[:, lo:lo + QKV_COLS]
        for j in range(QKV_COLS // V7X_LANES):
            col = qc[:, j * V7X_LANES:(j + 1) * V7X_LANES]
            q_ref[:, lo + j * V7X_LANES:lo + (j + 1) * V7X_LANES] = (
                _rope(col, cos_q, sin_q, is_low).astype(BF16))

    kvc = jnp.dot(xn_ref[...], wqkv_ref[:, Q_WIDTH:Q_WIDTH + 2 * KV_WIDTH],
                  preferred_element_type=F32) + bqkv_ref[:, Q_WIDTH:Q_WIDTH + 2 * KV_WIDTH]
    for j in range(KV_WIDTH // V7X_LANES):
        kj = _rope(kvc[:, j * V7X_LANES:(j + 1) * V7X_LANES], cos_k, sin_k, is_low)
        swapped = pltpu.roll(kj, HEAD_DIM, axis=1)
        even = jnp.where(first_head, kj, swapped)
        odd = jnp.where(first_head, swapped, kj)
        kd_ref[ATTN_BLOCK:ATTN_BLOCK + tm,
               (2 * j) * V7X_LANES:(2 * j + 1) * V7X_LANES] = even.astype(BF16)
        kd_ref[ATTN_BLOCK:ATTN_BLOCK + tm,
               (2 * j + 1) * V7X_LANES:(2 * j + 2) * V7X_LANES] = odd.astype(BF16)
    vt_ref[:, ATTN_BLOCK:ATTN_BLOCK + tm] = kvc[:, KV_WIDTH:2 * KV_WIDTH].T.astype(BF16)

    cols4 = GQA_GROUP * Q_SUB
    sj = lax.broadcasted_iota(jnp.int32, (SUB_KEYS, cols4), 0)
    qi = lax.broadcasted_iota(jnp.int32, (SUB_KEYS, cols4), 1) % Q_SUB
    diff = ATTN_BLOCK + qi - sj
    band = (diff >= 0) & (diff < WINDOW)
    no_prev = seq_tile == 0
    band_first = [band & (sj >= jnp.where(no_prev, ATTN_BLOCK - u * Q_SUB, 0))
                  for u in range(ATTN_BLOCK // Q_SUB)]
    half = lax.broadcasted_iota(jnp.int32, (Q_SUB, V7X_LANES), 1) < HEAD_DIM
    zero_q = jnp.zeros((Q_SUB, V7X_LANES), BF16)
    head_of_lane = lax.broadcasted_iota(jnp.int32, (1, cols4), 1) // Q_SUB
    ones_rows = jnp.ones((SUM_ROWS, 2 * ATTN_BLOCK), BF16)
    zero_keys = jnp.zeros((2 * ATTN_BLOCK - SUB_KEYS, cols4), BF16)

    sinks = []
    for kv in range(N_KV_HEADS):
        sink = jnp.zeros((1, cols4), F32)
        for e in range(GQA_GROUP):
            sink = jnp.where(head_of_lane == e, sinks_ref[GQA_GROUP * kv + e], sink)
        sinks.append(sink * LOG2_E)

    def scores(kv, u):
        rows = slice(u * Q_SUB, (u + 1) * Q_SUB)
        keys = slice(u * Q_SUB, u * Q_SUB + SUB_KEYS)
        stacked = []
        for pair in range(GQA_GROUP // 2):
            qcols = slice((2 * kv + pair) * V7X_LANES, (2 * kv + pair + 1) * V7X_LANES)
            qp = q_ref[rows, qcols]
            stacked.append(jnp.where(half, qp, zero_q))
            stacked.append(jnp.where(half, zero_q, qp))
        q4 = jnp.concatenate(stacked, axis=0)
        return lax.dot_general(kd_ref[keys, kv * V7X_LANES:(kv + 1) * V7X_LANES], q4,
                               (((1,), (1,)), ((), ())), preferred_element_type=F32)

    def finish(kv, u, s):
        block = u // (ATTN_BLOCK // Q_SUB)
        part = u % (ATTN_BLOCK // Q_SUB)
        rows = slice(u * Q_SUB, (u + 1) * Q_SUB)
        window = slice(block * ATTN_BLOCK, (block + 2) * ATTN_BLOCK)
        valid = band_first[u] if block == 0 else band
        sink = sinks[kv]
        s = jnp.where(valid, s, NEG_INF)
        m = jnp.maximum(jnp.max(s, axis=0, keepdims=True), sink)
        p = jnp.exp2(s - m).astype(BF16)
        p_win = jnp.concatenate([p, zero_keys] if part == 0 else [zero_keys, p], axis=0)
        v_aug = jnp.concatenate(
            [vt_ref[kv * HEAD_DIM:(kv + 1) * HEAD_DIM, window], ones_rows], axis=0)
        ot = jnp.dot(v_aug, p_win, preferred_element_type=F32)
        denom = ot[HEAD_DIM:HEAD_DIM + 1, :] + jnp.exp2(sink - m)
        ot = ot[0:HEAD_DIM, :] * (1.0 / denom)
        for e in range(GQA_GROUP):
            head = GQA_GROUP * kv + e
            ot_ref[head * HEAD_DIM:(head + 1) * HEAD_DIM, rows] = (
                ot[:, e * Q_SUB:(e + 1) * Q_SUB])

    units = [(kv, u) for kv in range(N_KV_HEADS) for u in range(tm // Q_SUB)]
    in_flight = [scores(*unit) for unit in units[:SCORE_LOOKAHEAD]]
    for i, unit in enumerate(units):
        if i + SCORE_LOOKAHEAD < len(units):
            in_flight.append(scores(*units[i + SCORE_LOOKAHEAD]))
        finish(*unit, in_flight.pop(0))

    kd_ref[0:ATTN_BLOCK, :] = kd_ref[tm:tm + ATTN_BLOCK, :]
    vt_ref[:, 0:ATTN_BLOCK] = vt_ref[:, tm:tm + ATTN_BLOCK]

    att = ot_ref[...].T.astype(BF16)
    mix = jnp.dot(att, wo_ref[...], preferred_element_type=F32)
    o_ref[...] = h + _rmsnorm(mix, post_ref[...])


def _attn_layer(h, cos_tab, sin_tab, pre_g, post_g, w_qkv, b_qkv, sinks, w_o,
                batch, seq):
    tokens = h.shape[0]
    tm = ATTN_TILE
    tiles = seq // tm
    buffers = (4 * tm * D_MODEL * 4 + 4 * tm * V7X_LANES * 4
               + w_qkv.size * 2 + w_o.size * 2 + 2 * tm * D_MODEL * 2
               + (tm + ATTN_BLOCK) * (KV_DUP_WIDTH + KV_WIDTH) * 2
               + tm * Q_WIDTH * 4)
    tile_map = lambda b, i: (b * tiles + i, 0)
    return pl.pallas_call(
        _attn_body,
        out_shape=jax.ShapeDtypeStruct((tokens, D_MODEL), F32),
        grid=(batch, tiles),
        in_specs=[
            pl.BlockSpec(memory_space=pltpu.SMEM),
            pl.BlockSpec((tm, D_MODEL), tile_map),
            pl.BlockSpec((tm, V7X_LANES), tile_map),
            pl.BlockSpec((tm, V7X_LANES), tile_map),
            _resident((1, D_MODEL)),
            _resident((1, D_MODEL)),
            _resident(w_qkv.shape),
            _resident(b_qkv.shape),
            _resident(w_o.shape),
        ],
        out_specs=pl.BlockSpec((tm, D_MODEL), tile_map),
        scratch_shapes=[
            pltpu.VMEM((tm, D_MODEL), BF16),
            pltpu.VMEM((tm, Q_WIDTH), BF16),
            pltpu.VMEM((tm + ATTN_BLOCK, KV_DUP_WIDTH), BF16),
            pltpu.VMEM((KV_WIDTH, tm + ATTN_BLOCK), BF16),
            pltpu.VMEM((Q_WIDTH, tm), F32),
        ],
        compiler_params=pltpu.CompilerParams(
            dimension_semantics=("arbitrary", "arbitrary"),
            vmem_limit_bytes=_vmem_limit(buffers)),
        name="swa",
    )(sinks, h, cos_tab, sin_tab, pre_g, post_g, w_qkv, b_qkv, w_o)


def _rope_lane_tables(positions):
    half = ROPE_DIM // 2
    inv_freq = ROPE_THETA ** (-jnp.arange(0, ROPE_DIM, 2, dtype=F32) / ROPE_DIM)
    ang = positions.astype(F32).reshape(-1, 1) * inv_freq
    cos, sin = jnp.cos(ang), jnp.sin(ang)
    tokens = ang.shape[0]
    rest = HEAD_DIM - ROPE_DIM
    cos_head = jnp.concatenate([cos, cos, jnp.ones((tokens, rest), F32)], axis=1)
    sin_head = jnp.concatenate([-sin, sin, jnp.zeros((tokens, rest), F32)], axis=1)
    reps = V7X_LANES // HEAD_DIM
    return jnp.tile(cos_head, (1, reps)), jnp.tile(sin_head, (1, reps))


def kernel(x, positions, pre_mix_g, post_mix_g, pre_ffn_g, post_ffn_g,
           a_w_in, a_b_in, a_ln_g, a_ln_b, a_w_s, a_b_s, a_w_out,
           b_w_qkv, b_b_qkv, b_sinks, b_w_o, ffn_w_gu, ffn_w_down):
    batch, seq, d_model = x.shape
    depth = pre_mix_g.shape[0]
    assert d_model == D_MODEL
    assert seq % ATTN_TILE == 0 and seq % GMLP_TILE == 0 and (batch * seq) % FFN_TILE == 0
    cos_tab, sin_tab = _rope_lane_tables(positions)
    row = lambda v: v.reshape(1, -1)
    h = x.reshape(batch * seq, d_model)
    for i in range(depth):
        j = i // 2
        if i % 2 == 0:
            h = _gmlp_layer(h, row(pre_mix_g[i]), row(post_mix_g[i]),
                            a_w_in[j].astype(BF16), row(a_b_in[j]),
                            row(a_ln_g[j]), row(a_ln_b[j]),
                            a_w_s[j], a_b_s[j].T, a_w_out[j].astype(BF16))
        else:
            h = _attn_layer(h, cos_tab, sin_tab, row(pre_mix_g[i]), row(post_mix_g[i]),
                            b_w_qkv[j].astype(BF16), row(b_b_qkv[j]), b_sinks[j],
                            b_w_o[j].astype(BF16), batch, seq)
        h = _ffn_layer(h, row(pre_ffn_g[i]), row(post_ffn_g[i]),
                       ffn_w_gu[i].astype(BF16), ffn_w_down[i].astype(BF16))
    return h.reshape(batch, seq, d_model)
```

```python
import functools

import jax
import jax.numpy as jnp
from jax import lax
from jax.experimental import pallas as pl
from jax.experimental.pallas import tpu as pltpu

D_MODEL = 1024
CHUNK = 128
GMLP_WIDTH = 2 * D_MODEL
GMLP_GROUPS = 8
GMLP_GROUP_DIM = GMLP_WIDTH // GMLP_GROUPS
HEAD_DIM = 64
N_Q_HEADS = D_MODEL // HEAD_DIM
N_KV_HEADS = 4
GQA_GROUP = N_Q_HEADS // N_KV_HEADS
WINDOW = 128
ATTN_BLOCK = 128
ROPE_DIM = HEAD_DIM // 4
ROPE_THETA = 500000.0
Q_WIDTH = N_Q_HEADS * HEAD_DIM
KV_WIDTH = N_KV_HEADS * HEAD_DIM
FFN_HIDDEN = -(-(8 * D_MODEL) // (3 * 256)) * 256
RMS_EPS = 1e-6
LN_EPS = 1e-5
NEG_INF = -1e30

V7X_LANES = 128
V7X_VMEM_BYTES = 64 * 1024 * 1024

F32 = jnp.float32
BF16 = jnp.bfloat16

FFN_TILE = 1024
GMLP_TILE = 1024
ATTN_TILE = 1024
ROW_SLAB = 256
RING_SLABS = 2
GMLP_IN_COLS = 512
GATE_GROUPS_PER_TASK = 2
FFN_HIDDEN_COLS = 256


def _vmem_limit(buffer_bytes):
    return int(min(2 * buffer_bytes, V7X_VMEM_BYTES - 8 * 1024 * 1024))


def _rmsnorm(x, g):
    ms = jnp.mean(x * x, axis=-1, keepdims=True)
    return x * lax.rsqrt(ms + RMS_EPS) * g


def _resident(shape):
    zeros = (0,) * len(shape)
    return pl.BlockSpec(shape, lambda *_: zeros)


def _interleave(*task_lists):
    keyed = [((i + 0.5) / len(tasks), k, i, task)
             for k, tasks in enumerate(task_lists) for i, task in enumerate(tasks)]
    return [entry[3] for entry in sorted(keyed, key=lambda entry: entry[:3])]


def _layer_resident(stack_shape, layer):
    index = (layer,) + (0,) * (len(stack_shape) - 1)
    return pl.BlockSpec((None,) + tuple(stack_shape[1:]), lambda *_: index,
                        pipeline_mode=pl.Buffered(1))


def _ffn_body(layer, h_ref, pre_ref, post_ref, wgu_ref, wd_ref, o_ref, xn_ref, act_ref):
    n_slabs = h_ref.shape[0] // ROW_SLAB

    def rows(s):
        return slice(s * ROW_SLAB, (s + 1) * ROW_SLAB)

    def prologue(s):
        xn_ref[rows(s), :] = _rmsnorm(h_ref[rows(s), :], pre_ref[layer:layer + 1, :]).astype(BF16)

    def gate_up(s):
        xn = xn_ref[rows(s), :]
        for c in range(FFN_HIDDEN // FFN_HIDDEN_COLS):
            lo = c * FFN_HIDDEN_COLS
            gate = jnp.dot(xn, wgu_ref[:, lo:lo + FFN_HIDDEN_COLS],
                           preferred_element_type=F32)
            up = jnp.dot(xn, wgu_ref[:, FFN_HIDDEN + lo:FFN_HIDDEN + lo + FFN_HIDDEN_COLS],
                         preferred_element_type=F32)
            act = gate * (1.0 / (1.0 + jnp.exp(-gate))) * up
            act_ref[rows(s), lo:lo + FFN_HIDDEN_COLS] = act.astype(BF16)

    def down(s):
        f = jnp.dot(act_ref[rows(s), :], wd_ref[...], preferred_element_type=F32)
        o_ref[rows(s), :] = h_ref[rows(s), :] + _rmsnorm(f, post_ref[layer:layer + 1, :])

    prologue(0)
    for s in range(n_slabs):
        if s + 1 < n_slabs:
            prologue(s + 1)
        gate_up(s)
        if s > 0:
            down(s - 1)
    down(n_slabs - 1)


def _ffn_layer(layer, h, pre_g, post_g, w_gu, w_down):
    tokens = h.shape[0]
    tm = FFN_TILE
    buffers = (4 * tm * D_MODEL * 4 + w_gu[0].size * 2 + w_down[0].size * 2
               + tm * D_MODEL * 2 + tm * FFN_HIDDEN * 2)
    return pl.pallas_call(
        functools.partial(_ffn_body, layer),
        out_shape=jax.ShapeDtypeStruct((tokens, D_MODEL), F32),
        grid=(tokens // tm,),
        in_specs=[
            pl.BlockSpec((tm, D_MODEL), lambda i: (i, 0)),
            _resident(pre_g.shape),
            _resident(post_g.shape),
            _layer_resident(w_gu.shape, layer),
            _layer_resident(w_down.shape, layer),
        ],
        out_specs=pl.BlockSpec((tm, D_MODEL), lambda i: (i, 0)),
        scratch_shapes=[
            pltpu.VMEM((tm, D_MODEL), BF16),
            pltpu.VMEM((tm, FFN_HIDDEN), BF16),
        ],
        compiler_params=pltpu.CompilerParams(
            dimension_semantics=("arbitrary",),
            vmem_limit_bytes=_vmem_limit(buffers)),
        name="ffn",
    )(h, pre_g, post_g, w_gu, w_down)


def _gmlp_body(layer, mixer, h_ref, pre_ref, post_ref, win_ref, bin_ref, lng_ref, lnb_ref,
               ws_ref, bst_ref, wout_ref, o_ref,
               xn_ref, u_ref, v_ref, vn_ref, gated_ref):
    n_slabs = h_ref.shape[0] // ROW_SLAB

    def rows(s):
        return slice(s * ROW_SLAB, (s + 1) * ROW_SLAB)

    def ring(s):
        return rows(s % RING_SLABS)

    def prologue(s):
        xn_ref[rows(s), :] = _rmsnorm(h_ref[rows(s), :], pre_ref[layer:layer + 1, :]).astype(BF16)

    def project_in(s, c):
        lo = c * GMLP_IN_COLS
        z = jnp.dot(xn_ref[rows(s), :], win_ref[:, lo:lo + GMLP_IN_COLS],
                    preferred_element_type=F32) + bin_ref[mixer:mixer + 1, lo:lo + GMLP_IN_COLS]
        z2 = z * (1.0 + lax.erf(z * 0.7071067811865476))
        if lo < GMLP_WIDTH:
            u_ref[ring(s), lo:lo + GMLP_IN_COLS] = z2
        else:
            v_ref[ring(s), lo - GMLP_WIDTH:lo - GMLP_WIDTH + GMLP_IN_COLS] = z2

    row = lax.broadcasted_iota(jnp.int32, (CHUNK, CHUNK), 0)
    col = lax.broadcasted_iota(jnp.int32, (CHUNK, CHUNK), 1)
    causal = col <= row
    half_w_spatial = [jnp.where(causal, 0.5 * ws_ref[g], 0.0).astype(BF16)
                      for g in range(GMLP_GROUPS)]
    half_b_spatial = 0.5 * bst_ref[...]

    def chunk_rows(s, r):
        base = (s % RING_SLABS) * ROW_SLAB + r * CHUNK
        return slice(base, base + CHUNK)

    def layernorm(s, r):
        v2 = v_ref[chunk_rows(s, r), :]
        mu = jnp.mean(v2, axis=-1, keepdims=True)
        d = v2 - mu
        var = jnp.mean(d * d, axis=-1, keepdims=True)
        vn_ref[chunk_rows(s, r), :] = (
            d * lax.rsqrt(var + 4.0 * LN_EPS) * lng_ref[mixer:mixer + 1, :]
            + lnb_ref[mixer:mixer + 1, :]).astype(BF16)

    def spatial_gate(s, r, groups):
        chunk = chunk_rows(s, r)
        for g in groups:
            cols = slice(g * GMLP_GROUP_DIM, (g + 1) * GMLP_GROUP_DIM)
            half_sv = jnp.dot(half_w_spatial[g], vn_ref[chunk, cols], preferred_element_type=F32)
            half_sv = half_sv + half_b_spatial[:, g:g + 1]
            gated_ref[chunk, cols] = (u_ref[chunk, cols] * half_sv).astype(BF16)

    def project_out(s):
        mix = jnp.dot(gated_ref[ring(s), :], wout_ref[...], preferred_element_type=F32)
        o_ref[rows(s), :] = h_ref[rows(s), :] + _rmsnorm(mix, post_ref[layer:layer + 1, :])

    def in_tasks(s):
        return [functools.partial(prologue, s)] + [
            functools.partial(project_in, s, c) for c in range(2 * GMLP_WIDTH // GMLP_IN_COLS)]

    def gate_out_tasks(s):
        tasks = []
        for r in range(ROW_SLAB // CHUNK):
            tasks.append(functools.partial(layernorm, s, r))
            for g in range(0, GMLP_GROUPS, GATE_GROUPS_PER_TASK):
                tasks.append(functools.partial(spatial_gate, s, r,
                                               range(g, g + GATE_GROUPS_PER_TASK)))
        return tasks + [functools.partial(project_out, s)]

    for task in in_tasks(0):
        task()
    for s in range(n_slabs):
        ahead = in_tasks(s + 1) if s + 1 < n_slabs else []
        for task in _interleave(gate_out_tasks(s), ahead):
            task()


def _gmlp_layer(layer, mixer, h, pre_g, post_g, w_in, b_in, ln_g, ln_b, w_s, b_s_t, w_out):
    tokens = h.shape[0]
    tm = GMLP_TILE
    ring_rows = RING_SLABS * ROW_SLAB
    buffers = (4 * tm * D_MODEL * 4 + w_in[0].size * 2 + w_out[0].size * 2
               + tm * D_MODEL * 2 + 2 * ring_rows * GMLP_WIDTH * 4
               + 2 * ring_rows * GMLP_WIDTH * 2)
    return pl.pallas_call(
        functools.partial(_gmlp_body, layer, mixer),
        out_shape=jax.ShapeDtypeStruct((tokens, D_MODEL), F32),
        grid=(tokens // tm,),
        in_specs=[
            pl.BlockSpec((tm, D_MODEL), lambda i: (i, 0)),
            _resident(pre_g.shape),
            _resident(post_g.shape),
            _layer_resident(w_in.shape, mixer),
            _resident(b_in.shape),
            _resident(ln_g.shape),
            _resident(ln_b.shape),
            _layer_resident(w_s.shape, mixer),
            _layer_resident(b_s_t.shape, mixer),
            _layer_resident(w_out.shape, mixer),
        ],
        out_specs=pl.BlockSpec((tm, D_MODEL), lambda i: (i, 0)),
        scratch_shapes=[
            pltpu.VMEM((tm, D_MODEL), BF16),
            pltpu.VMEM((ring_rows, GMLP_WIDTH), F32),
            pltpu.VMEM((ring_rows, GMLP_WIDTH), F32),
            pltpu.VMEM((ring_rows, GMLP_WIDTH), BF16),
            pltpu.VMEM((ring_rows, GMLP_WIDTH), BF16),
        ],
        compiler_params=pltpu.CompilerParams(
            dimension_semantics=("arbitrary",),
            vmem_limit_bytes=_vmem_limit(buffers)),
        name="gmlp",
    )(h, pre_g, post_g, w_in, b_in, ln_g, ln_b, w_s, b_s_t, w_out)


KV_DUP_WIDTH = 2 * KV_WIDTH
QKV_COLS = 512
Q_SUB = 64
SUB_KEYS = Q_SUB + WINDOW
SUM_ROWS = 16
LOG2_E = 1.4426950408889634
SCORE_LOOKAHEAD = 4


def _rope(x, cos_tab, sin_tab, is_low):
    partner = jnp.where(is_low,
                        pltpu.roll(x, V7X_LANES - ROPE_DIM // 2, axis=1),
                        pltpu.roll(x, ROPE_DIM // 2, axis=1))
    return x * cos_tab + partner * sin_tab


def _attn_body(layer, mixer, sinks_ref, h_ref, cos_ref, sin_ref, pre_ref, post_ref,
               wqkv_ref, bqkv_ref, wo_ref, o_ref,
               xn_ref, q_ref, kd_ref, vt_ref, ot_ref):
    tm = h_ref.shape[0]
    n_slabs = tm // ROW_SLAB
    seq_tile = pl.program_id(1)

    @pl.when(seq_tile == 0)
    def _():
        kd_ref[0:ATTN_BLOCK, :] = jnp.zeros((ATTN_BLOCK, KV_DUP_WIDTH), BF16)
        vt_ref[:, 0:ATTN_BLOCK] = jnp.zeros((KV_WIDTH, ATTN_BLOCK), BF16)

    lane = lax.broadcasted_iota(jnp.int32, (ROW_SLAB, V7X_LANES), 1)
    first_head = lane < HEAD_DIM
    is_low = (lane % HEAD_DIM) < (ROPE_DIM // 2)

    def slab_rows(s):
        return slice(s * ROW_SLAB, (s + 1) * ROW_SLAB)

    def kv_rows(s):
        return slice(ATTN_BLOCK + s * ROW_SLAB, ATTN_BLOCK + (s + 1) * ROW_SLAB)

    def norm_task(s):
        xn_ref[slab_rows(s), :] = _rmsnorm(h_ref[slab_rows(s), :], pre_ref[layer:layer + 1, :]).astype(BF16)

    def q_task(s, c):
        cos_q = cos_ref[slab_rows(s), :] * (HEAD_DIM ** -0.5 * LOG2_E)
        sin_q = sin_ref[slab_rows(s), :] * (HEAD_DIM ** -0.5 * LOG2_E)
        lo = c * QKV_COLS
        qc = jnp.dot(xn_ref[slab_rows(s), :], wqkv_ref[:, lo:lo + QKV_COLS],
                     preferred_element_type=F32) + bqkv_ref[mixer:mixer + 1, lo:lo + QKV_COLS]
        for j in range(QKV_COLS // V7X_LANES):
            col = qc[:, j * V7X_LANES:(j + 1) * V7X_LANES]
            q_ref[slab_rows(s), lo + j * V7X_LANES:lo + (j + 1) * V7X_LANES] = (
                _rope(col, cos_q, sin_q, is_low).astype(BF16))

    def kv_task(s):
        cos_k = cos_ref[slab_rows(s), :]
        sin_k = sin_ref[slab_rows(s), :]
        kvc = (jnp.dot(xn_ref[slab_rows(s), :], wqkv_ref[:, Q_WIDTH:Q_WIDTH + 2 * KV_WIDTH],
                       preferred_element_type=F32)
               + bqkv_ref[mixer:mixer + 1, Q_WIDTH:Q_WIDTH + 2 * KV_WIDTH])
        for j in range(KV_WIDTH // V7X_LANES):
            kj = _rope(kvc[:, j * V7X_LANES:(j + 1) * V7X_LANES], cos_k, sin_k, is_low)
            swapped = pltpu.roll(kj, HEAD_DIM, axis=1)
            even = jnp.where(first_head, kj, swapped)
            odd = jnp.where(first_head, swapped, kj)
            kd_ref[kv_rows(s), (2 * j) * V7X_LANES:(2 * j + 1) * V7X_LANES] = even.astype(BF16)
            kd_ref[kv_rows(s), (2 * j + 1) * V7X_LANES:(2 * j + 2) * V7X_LANES] = odd.astype(BF16)
        vt_ref[:, kv_rows(s)] = kvc[:, KV_WIDTH:2 * KV_WIDTH].T.astype(BF16)

    def out_task(s):
        att = ot_ref[:, slab_rows(s)].T.astype(BF16)
        mix = jnp.dot(att, wo_ref[...], preferred_element_type=F32)
        o_ref[slab_rows(s), :] = h_ref[slab_rows(s), :] + _rmsnorm(mix, post_ref[layer:layer + 1, :])

    cols4 = GQA_GROUP * Q_SUB
    sj = lax.broadcasted_iota(jnp.int32, (SUB_KEYS, cols4), 0)
    qi = lax.broadcasted_iota(jnp.int32, (SUB_KEYS, cols4), 1) % Q_SUB
    diff = ATTN_BLOCK + qi - sj
    band = (diff >= 0) & (diff < WINDOW)
    no_prev = seq_tile == 0
    band_first = [band & (sj >= jnp.where(no_prev, ATTN_BLOCK - u * Q_SUB, 0))
                  for u in range(ATTN_BLOCK // Q_SUB)]
    half = lax.broadcasted_iota(jnp.int32, (Q_SUB, V7X_LANES), 1) < HEAD_DIM
    zero_q = jnp.zeros((Q_SUB, V7X_LANES), BF16)
    head_of_lane = lax.broadcasted_iota(jnp.int32, (1, cols4), 1) // Q_SUB
    ones_rows = jnp.ones((SUM_ROWS, 2 * ATTN_BLOCK), BF16)
    zero_keys = jnp.zeros((2 * ATTN_BLOCK - SUB_KEYS, cols4), BF16)

    sinks = []
    for kv in range(N_KV_HEADS):
        sink = jnp.zeros((1, cols4), F32)
        for e in range(GQA_GROUP):
            sink = jnp.where(head_of_lane == e, sinks_ref[mixer, GQA_GROUP * kv + e], sink)
        sinks.append(sink * LOG2_E)

    def scores(kv, u):
        rows = slice(u * Q_SUB, (u + 1) * Q_SUB)
        keys = slice(u * Q_SUB, u * Q_SUB + SUB_KEYS)
        stacked = []
        for pair in range(GQA_GROUP // 2):
            qcols = slice((2 * kv + pair) * V7X_LANES, (2 * kv + pair + 1) * V7X_LANES)
            qp = q_ref[rows, qcols]
            stacked.append(jnp.where(half, qp, zero_q))
            stacked.append(jnp.where(half, zero_q, qp))
        q4 = jnp.concatenate(stacked, axis=0)
        return lax.dot_general(kd_ref[keys, kv * V7X_LANES:(kv + 1) * V7X_LANES], q4,
                               (((1,), (1,)), ((), ())), preferred_element_type=F32)

    def finish(kv, u, s):
        block = u // (ATTN_BLOCK // Q_SUB)
        part = u % (ATTN_BLOCK // Q_SUB)
        rows = slice(u * Q_SUB, (u + 1) * Q_SUB)
        window = slice(block * ATTN_BLOCK, (block + 2) * ATTN_BLOCK)
        valid = band_first[u] if block == 0 else band
        sink = sinks[kv]
        s = jnp.where(valid, s, NEG_INF)
        m = jnp.maximum(jnp.max(s, axis=0, keepdims=True), sink)
        p = jnp.exp2(s - m).astype(BF16)
        p_win = jnp.concatenate([p, zero_keys] if part == 0 else [zero_keys, p], axis=0)
        v_aug = jnp.concatenate(
            [vt_ref[kv * HEAD_DIM:(kv + 1) * HEAD_DIM, window], ones_rows], axis=0)
        ot = jnp.dot(v_aug, p_win, preferred_element_type=F32)
        denom = ot[HEAD_DIM:HEAD_DIM + 1, :] + jnp.exp2(sink - m)
        ot = ot[0:HEAD_DIM, :] * (1.0 / denom)
        for e in range(GQA_GROUP):
            head = GQA_GROUP * kv + e
            ot_ref[head * HEAD_DIM:(head + 1) * HEAD_DIM, rows] = (
                ot[:, e * Q_SUB:(e + 1) * Q_SUB])

    subs = ROW_SLAB // Q_SUB
    order = []
    for s in range(n_slabs):
        units = [(kv, u) for kv in range(N_KV_HEADS) for u in range(s * subs, (s + 1) * subs)]
        tasks = []
        if s + 1 < n_slabs:
            tasks += [functools.partial(norm_task, s + 1)]
            tasks += [functools.partial(q_task, s + 1, c) for c in range(Q_WIDTH // QKV_COLS)]
            tasks += [functools.partial(kv_task, s + 1)]
        if s > 0:
            tasks += [functools.partial(out_task, s - 1)]
        span = len(units) - SCORE_LOOKAHEAD
        after = {}
        for t, task in enumerate(tasks):
            after.setdefault(max((t + 1) * span // (len(tasks) + 1) - 1, 0), []).append(task)
        for i, unit in enumerate(units):
            order.append(unit)
            order.extend(after.get(i, []))
    unit_list = [e for e in order if isinstance(e, tuple)]

    norm_task(0)
    for c in range(Q_WIDTH // QKV_COLS):
        q_task(0, c)
    kv_task(0)
    scored = []
    done = 0
    for entry in order:
        if not isinstance(entry, tuple):
            entry()
            continue
        ahead = min(done + 1 + SCORE_LOOKAHEAD, len(unit_list))
        while done + len(scored) < ahead:
            scored.append(scores(*unit_list[done + len(scored)]))
        finish(*entry, scored.pop(0))
        done += 1
    out_task(n_slabs - 1)

    kd_ref[0:ATTN_BLOCK, :] = kd_ref[tm:tm + ATTN_BLOCK, :]
    vt_ref[:, 0:ATTN_BLOCK] = vt_ref[:, tm:tm + ATTN_BLOCK]


def _attn_layer(layer, mixer, h, cos_tab, sin_tab, pre_g, post_g, w_qkv, b_qkv, sinks, w_o,
                batch, seq):
    tokens = h.shape[0]
    tm = ATTN_TILE
    tiles = seq // tm
    buffers = (4 * tm * D_MODEL * 4 + 4 * tm * V7X_LANES * 4
               + w_qkv[0].size * 2 + w_o[0].size * 2 + 2 * tm * D_MODEL * 2
               + (tm + ATTN_BLOCK) * (KV_DUP_WIDTH + KV_WIDTH) * 2
               + tm * Q_WIDTH * 4)
    tile_map = lambda b, i: (b * tiles + i, 0)
    return pl.pallas_call(
        functools.partial(_attn_body, layer, mixer),
        out_shape=jax.ShapeDtypeStruct((tokens, D_MODEL), F32),
        grid=(batch, tiles),
        in_specs=[
            pl.BlockSpec(memory_space=pltpu.SMEM),
            pl.BlockSpec((tm, D_MODEL), tile_map),
            pl.BlockSpec((tm, V7X_LANES), tile_map),
            pl.BlockSpec((tm, V7X_LANES), tile_map),
            _resident(pre_g.shape),
            _resident(post_g.shape),
            _layer_resident(w_qkv.shape, mixer),
            _resident(b_qkv.shape),
            _layer_resident(w_o.shape, mixer),
        ],
        out_specs=pl.BlockSpec((tm, D_MODEL), tile_map),
        scratch_shapes=[
            pltpu.VMEM((tm, D_MODEL), BF16),
            pltpu.VMEM((tm, Q_WIDTH), BF16),
            pltpu.VMEM((tm + ATTN_BLOCK, KV_DUP_WIDTH), BF16),
            pltpu.VMEM((KV_WIDTH, tm + ATTN_BLOCK), BF16),
            pltpu.VMEM((Q_WIDTH, tm), F32),
        ],
        compiler_params=pltpu.CompilerParams(
            dimension_semantics=("arbitrary", "arbitrary"),
            vmem_limit_bytes=_vmem_limit(buffers)),
        name="swa",
    )(sinks, h, cos_tab, sin_tab, pre_g, post_g, w_qkv, b_qkv, w_o)


def _rope_lane_tables(positions):
    half = ROPE_DIM // 2
    inv_freq = ROPE_THETA ** (-jnp.arange(0, ROPE_DIM, 2, dtype=F32) / ROPE_DIM)
    ang = positions.astype(F32).reshape(-1, 1) * inv_freq
    cos, sin = jnp.cos(ang), jnp.sin(ang)
    tokens = ang.shape[0]
    rest = HEAD_DIM - ROPE_DIM
    cos_head = jnp.concatenate([cos, cos, jnp.ones((tokens, rest), F32)], axis=1)
    sin_head = jnp.concatenate([-sin, sin, jnp.zeros((tokens, rest), F32)], axis=1)
    reps = V7X_LANES // HEAD_DIM
    return jnp.tile(cos_head, (1, reps)), jnp.tile(sin_head, (1, reps))


def kernel(x, positions, pre_mix_g, post_mix_g, pre_ffn_g, post_ffn_g,
           a_w_in, a_b_in, a_ln_g, a_ln_b, a_w_s, a_b_s, a_w_out,
           b_w_qkv, b_b_qkv, b_sinks, b_w_o, ffn_w_gu, ffn_w_down):
    batch, seq, d_model = x.shape
    depth = pre_mix_g.shape[0]
    assert d_model == D_MODEL
    assert seq % ATTN_TILE == 0 and seq % GMLP_TILE == 0 and (batch * seq) % FFN_TILE == 0
    cos_tab, sin_tab = _rope_lane_tables(positions)
    a_w_in, a_w_out, b_w_qkv, b_w_o, ffn_w_gu, ffn_w_down = (
        w.astype(BF16) for w in (a_w_in, a_w_out, b_w_qkv, b_w_o, ffn_w_gu, ffn_w_down))
    a_b_s_t = jnp.swapaxes(a_b_s, 1, 2)
    h = x.reshape(batch * seq, d_model)
    for i in range(depth):
        j = i // 2
        if i % 2 == 0:
            h = _gmlp_layer(i, j, h, pre_mix_g, post_mix_g, a_w_in, a_b_in, a_ln_g, a_ln_b,
                            a_w_s, a_b_s_t, a_w_out)
        else:
            h = _attn_layer(i, j, h, cos_tab, sin_tab, pre_mix_g, post_mix_g,
                            b_w_qkv, b_b_qkv, b_sinks, b_w_o, batch, seq)
        h = _ffn_layer(i, h, pre_ffn_g, post_ffn_g, ffn_w_gu, ffn_w_down)
    return h.reshape(batch, seq, d_model)
```

```python
import functools

import jax
import jax.numpy as jnp
from jax import lax
from jax.experimental import pallas as pl
from jax.experimental.pallas import tpu as pltpu

D_MODEL = 1024
CHUNK = 128
GMLP_WIDTH = 2 * D_MODEL
GMLP_GROUPS = 8
GMLP_GROUP_DIM = GMLP_WIDTH // GMLP_GROUPS
HEAD_DIM = 64
N_Q_HEADS = D_MODEL // HEAD_DIM
N_KV_HEADS = 4
GQA_GROUP = N_Q_HEADS // N_KV_HEADS
WINDOW = 128
ATTN_BLOCK = 128
ROPE_DIM = HEAD_DIM // 4
ROPE_THETA = 500000.0
Q_WIDTH = N_Q_HEADS * HEAD_DIM
KV_WIDTH = N_KV_HEADS * HEAD_DIM
FFN_HIDDEN = -(-(8 * D_MODEL) // (3 * 256)) * 256
RMS_EPS = 1e-6
LN_EPS = 1e-5
NEG_INF = -1e30

V7X_LANES = 128
V7X_VMEM_BYTES = 64 * 1024 * 1024

F32 = jnp.float32
BF16 = jnp.bfloat16

FFN_TILE = 1024
GMLP_TILE = 1024
ATTN_TILE = 1024
ROW_SLAB = 256
RING_SLABS = 2
GMLP_IN_COLS = 512
FFN_HIDDEN_COLS = 256


def _vmem_limit(buffer_bytes):
    return int(min(2 * buffer_bytes, V7X_VMEM_BYTES - 8 * 1024 * 1024))


def _rmsnorm(x, g):
    ms = jnp.mean(x * x, axis=-1, keepdims=True)
    return x * lax.rsqrt(ms + RMS_EPS) * g


def _resident(shape):
    zeros = (0,) * len(shape)
    return pl.BlockSpec(shape, lambda *_: zeros)


def _layer_resident(stack_shape, layer):
    index = (layer,) + (0,) * (len(stack_shape) - 1)
    return pl.BlockSpec((None,) + tuple(stack_shape[1:]), lambda *_: index,
                        pipeline_mode=pl.Buffered(1))


BF16_SUBLANES = 16
CAST_AHEAD = 1


class _CastJob:
    def __init__(self, stack, layer, steps):
        self.stack, self.layer = stack, layer
        _, self.rows, self.cols = stack.shape
        self.bands = steps
        while (self.rows % self.bands or (self.rows // self.bands) % BF16_SUBLANES):
            self.bands //= 2
        self.band_rows = self.rows // self.bands

    def in_spec(self, step_of):
        layer, last = self.layer, self.bands - 1
        return pl.BlockSpec((None, self.band_rows, self.cols),
                            lambda *ids: (layer, jnp.minimum(step_of(*ids), last), 0))

    def out_spec(self, step_of):
        last = self.bands - 1
        return pl.BlockSpec((self.band_rows, self.cols),
                            lambda *ids: (jnp.minimum(step_of(*ids), last), 0))

    def out_shape(self):
        return jax.ShapeDtypeStruct((self.rows, self.cols), BF16)

    def vmem_bytes(self):
        return 2 * self.band_rows * self.cols * (4 + 2)


def _cast_bands(step, jobs, src_refs, dst_refs):
    for job, src_ref, dst_ref in zip(jobs, src_refs, dst_refs):
        @pl.when(step < job.bands)
        def _(src_ref=src_ref, dst_ref=dst_ref):
            dst_ref[...] = src_ref[...].astype(BF16)


def _split_refs(refs, n_cast):
    return (refs[:n_cast], refs[n_cast], refs[n_cast + 1:2 * n_cast + 1],
            refs[2 * n_cast + 1:])


def _ffn_body(layer, jobs, h_ref, pre_ref, post_ref, wgu_ref, wd_ref, *refs):
    cast_src, o_ref, cast_dst, (xn_ref, act_ref) = _split_refs(refs, len(jobs))
    _cast_bands(pl.program_id(0), jobs, cast_src, cast_dst)
    n_slabs = h_ref.shape[0] // ROW_SLAB

    def rows(s):
        return slice(s * ROW_SLAB, (s + 1) * ROW_SLAB)

    def prologue(s):
        xn_ref[rows(s), :] = _rmsnorm(h_ref[rows(s), :], pre_ref[layer:layer + 1, :]).astype(BF16)

    def gate_up(s):
        xn = xn_ref[rows(s), :]
        for c in range(FFN_HIDDEN // FFN_HIDDEN_COLS):
            lo = c * FFN_HIDDEN_COLS
            gate = jnp.dot(xn, wgu_ref[:, lo:lo + FFN_HIDDEN_COLS],
                           preferred_element_type=F32)
            up = jnp.dot(xn, wgu_ref[:, FFN_HIDDEN + lo:FFN_HIDDEN + lo + FFN_HIDDEN_COLS],
                         preferred_element_type=F32)
            act = gate * (1.0 / (1.0 + jnp.exp(-gate))) * up
            act_ref[rows(s), lo:lo + FFN_HIDDEN_COLS] = act.astype(BF16)

    def down(s):
        f = jnp.dot(act_ref[rows(s), :], wd_ref[...], preferred_element_type=F32)
        o_ref[rows(s), :] = h_ref[rows(s), :] + _rmsnorm(f, post_ref[layer:layer + 1, :])

    prologue(0)
    for s in range(n_slabs):
        if s + 1 < n_slabs:
            prologue(s + 1)
        gate_up(s)
        if s > 0:
            down(s - 1)
    down(n_slabs - 1)


def _ffn_layer(layer, h, pre_g, post_g, w_gu, w_down, casts):
    tokens = h.shape[0]
    tm = FFN_TILE
    steps = tokens // tm
    step_of = lambda i: i
    jobs = [_CastJob(stack, idx, steps) for stack, idx in casts]
    buffers = (4 * tm * D_MODEL * 4 + w_gu.size * 2 + w_down.size * 2
               + tm * D_MODEL * 2 + tm * FFN_HIDDEN * 2
               + sum(job.vmem_bytes() for job in jobs))
    out = pl.pallas_call(
        functools.partial(_ffn_body, layer, jobs),
        out_shape=[jax.ShapeDtypeStruct((tokens, D_MODEL), F32)]
        + [job.out_shape() for job in jobs],
        grid=(steps,),
        in_specs=[
            pl.BlockSpec((tm, D_MODEL), lambda i: (i, 0)),
            _resident(pre_g.shape),
            _resident(post_g.shape),
            _resident(w_gu.shape),
            _resident(w_down.shape),
        ] + [job.in_spec(step_of) for job in jobs],
        out_specs=[pl.BlockSpec((tm, D_MODEL), lambda i: (i, 0))]
        + [job.out_spec(step_of) for job in jobs],
        scratch_shapes=[
            pltpu.VMEM((tm, D_MODEL), BF16),
            pltpu.VMEM((tm, FFN_HIDDEN), BF16),
        ],
        compiler_params=pltpu.CompilerParams(
            dimension_semantics=("arbitrary",),
            vmem_limit_bytes=_vmem_limit(buffers)),
        name="ffn",
    )(h, pre_g, post_g, w_gu, w_down, *[job.stack for job in jobs])
    return out[0], out[1:]


def _gmlp_body(layer, mixer, jobs, h_ref, pre_ref, post_ref, win_ref, bin_ref, lng_ref,
               lnb_ref, ws_ref, bst_ref, wout_ref, *refs):
    cast_src, o_ref, cast_dst, (xn_ref, u_ref, v_ref, gated_ref) = _split_refs(refs, len(jobs))
    _cast_bands(pl.program_id(0), jobs, cast_src, cast_dst)
    n_slabs = h_ref.shape[0] // ROW_SLAB

    def rows(s):
        return slice(s * ROW_SLAB, (s + 1) * ROW_SLAB)

    def ring(s):
        return rows(s % RING_SLABS)

    def prologue(s):
        xn_ref[rows(s), :] = _rmsnorm(h_ref[rows(s), :], pre_ref[layer:layer + 1, :]).astype(BF16)

    def project_in(s):
        xn = xn_ref[rows(s), :]
        for c in range(2 * GMLP_WIDTH // GMLP_IN_COLS):
            lo = c * GMLP_IN_COLS
            z = jnp.dot(xn, win_ref[:, lo:lo + GMLP_IN_COLS],
                        preferred_element_type=F32) + bin_ref[mixer:mixer + 1, lo:lo + GMLP_IN_COLS]
            z2 = z * (1.0 + lax.erf(z * 0.7071067811865476))
            if lo < GMLP_WIDTH:
                u_ref[ring(s), lo:lo + GMLP_IN_COLS] = z2
            else:
                v_ref[ring(s), lo - GMLP_WIDTH:lo - GMLP_WIDTH + GMLP_IN_COLS] = z2

    row = lax.broadcasted_iota(jnp.int32, (CHUNK, CHUNK), 0)
    col = lax.broadcasted_iota(jnp.int32, (CHUNK, CHUNK), 1)
    causal = col <= row
    half_w_spatial = [jnp.where(causal, 0.5 * ws_ref[g], 0.0).astype(BF16)
                      for g in range(GMLP_GROUPS)]
    half_b_spatial = 0.5 * bst_ref[...]

    def spatial_gate(s):
        base = (s % RING_SLABS) * ROW_SLAB
        for r in range(ROW_SLAB // CHUNK):
            chunk = slice(base + r * CHUNK, base + (r + 1) * CHUNK)
            v2 = v_ref[chunk, :]
            mu = jnp.mean(v2, axis=-1, keepdims=True)
            d = v2 - mu
            var = jnp.mean(d * d, axis=-1, keepdims=True)
            vn = (d * lax.rsqrt(var + 4.0 * LN_EPS) * lng_ref[mixer:mixer + 1, :]
                  + lnb_ref[mixer:mixer + 1, :]).astype(BF16)
            for g in range(GMLP_GROUPS):
                cols = slice(g * GMLP_GROUP_DIM, (g + 1) * GMLP_GROUP_DIM)
                half_sv = jnp.dot(half_w_spatial[g], vn[:, cols], preferred_element_type=F32)
                half_sv = half_sv + half_b_spatial[:, g:g + 1]
                gated_ref[chunk, cols] = (u_ref[chunk, cols] * half_sv).astype(BF16)

    def project_out(s):
        mix = jnp.dot(gated_ref[ring(s), :], wout_ref[...], preferred_element_type=F32)
        o_ref[rows(s), :] = h_ref[rows(s), :] + _rmsnorm(mix, post_ref[layer:layer + 1, :])

    prologue(0)
    project_in(0)
    for s in range(n_slabs):
        if s + 1 < n_slabs:
            prologue(s + 1)
            project_in(s + 1)
        spatial_gate(s)
        project_out(s)


def _gmlp_layer(layer, mixer, h, pre_g, post_g, w_in, b_in, ln_g, ln_b, w_s, b_s_t, w_out,
                casts):
    tokens = h.shape[0]
    tm = GMLP_TILE
    steps = tokens // tm
    step_of = lambda i: i
    jobs = [_CastJob(stack, idx, steps) for stack, idx in casts]
    ring_rows = RING_SLABS * ROW_SLAB
    buffers = (4 * tm * D_MODEL * 4 + w_in.size * 2 + w_out.size * 2
               + tm * D_MODEL * 2 + 2 * ring_rows * GMLP_WIDTH * 4
               + ring_rows * GMLP_WIDTH * 2 + sum(job.vmem_bytes() for job in jobs))
    out = pl.pallas_call(
        functools.partial(_gmlp_body, layer, mixer, jobs),
        out_shape=[jax.ShapeDtypeStruct((tokens, D_MODEL), F32)]
        + [job.out_shape() for job in jobs],
        grid=(steps,),
        in_specs=[
            pl.BlockSpec((tm, D_MODEL), lambda i: (i, 0)),
            _resident(pre_g.shape),
            _resident(post_g.shape),
            _resident(w_in.shape),
            _resident(b_in.shape),
            _resident(ln_g.shape),
            _resident(ln_b.shape),
            _layer_resident(w_s.shape, mixer),
            _layer_resident(b_s_t.shape, mixer),
            _resident(w_out.shape),
        ] + [job.in_spec(step_of) for job in jobs],
        out_specs=[pl.BlockSpec((tm, D_MODEL), lambda i: (i, 0))]
        + [job.out_spec(step_of) for job in jobs],
        scratch_shapes=[
            pltpu.VMEM((tm, D_MODEL), BF16),
            pltpu.VMEM((ring_rows, GMLP_WIDTH), F32),
            pltpu.VMEM((ring_rows, GMLP_WIDTH), F32),
            pltpu.VMEM((ring_rows, GMLP_WIDTH), BF16),
        ],
        compiler_params=pltpu.CompilerParams(
            dimension_semantics=("arbitrary",),
            vmem_limit_bytes=_vmem_limit(buffers)),
        name="gmlp",
    )(h, pre_g, post_g, w_in, b_in, ln_g, ln_b, w_s, b_s_t, w_out,
      *[job.stack for job in jobs])
    return out[0], out[1:]


KV_DUP_WIDTH = 2 * KV_WIDTH
QKV_COLS = 512
Q_SUB = 64
SUB_KEYS = Q_SUB + WINDOW
SUM_ROWS = 16
LOG2_E = 1.4426950408889634
SCORE_LOOKAHEAD = 4


def _rope(x, cos_tab, sin_tab, is_low):
    partner = jnp.where(is_low,
                        pltpu.roll(x, V7X_LANES - ROPE_DIM // 2, axis=1),
                        pltpu.roll(x, ROPE_DIM // 2, axis=1))
    return x * cos_tab + partner * sin_tab


def _attn_body(layer, mixer, jobs, sinks_ref, h_ref, rope_ref, pre_ref, post_ref,
               wqkv_ref, bqkv_ref, wo_ref, *refs):
    cast_src, o_ref, cast_dst, scratch = _split_refs(refs, len(jobs))
    xn_ref, q_ref, kd_ref, vt_ref, ot_ref, cos_ref, sin_ref = scratch
    _cast_bands(pl.program_id(0) * pl.num_programs(1) + pl.program_id(1),
                jobs, cast_src, cast_dst)
    tm = h_ref.shape[0]
    n_slabs = tm // ROW_SLAB
    seq_tile = pl.program_id(1)

    @pl.when(seq_tile == 0)
    def _():
        kd_ref[0:ATTN_BLOCK, :] = jnp.zeros((ATTN_BLOCK, KV_DUP_WIDTH), BF16)
        vt_ref[:, 0:ATTN_BLOCK] = jnp.zeros((KV_WIDTH, ATTN_BLOCK), BF16)

    lane = lax.broadcasted_iota(jnp.int32, (ROW_SLAB, V7X_LANES), 1)
    first_head = lane < HEAD_DIM
    is_low = (lane % HEAD_DIM) < (ROPE_DIM // 2)

    def slab_rows(s):
        return slice(s * ROW_SLAB, (s + 1) * ROW_SLAB)

    def kv_rows(s):
        return slice(ATTN_BLOCK + s * ROW_SLAB, ATTN_BLOCK + (s + 1) * ROW_SLAB)

    def norm_task(s):
        xn_ref[slab_rows(s), :] = _rmsnorm(h_ref[slab_rows(s), :],
                                           pre_ref[layer:layer + 1, :]).astype(BF16)
        packed = rope_ref[slab_rows(s), :]
        swapped = pltpu.roll(packed, HEAD_DIM, axis=1)
        cos_ref[slab_rows(s), :] = jnp.where(first_head, packed, swapped)
        sin_ref[slab_rows(s), :] = jnp.where(first_head, swapped, packed)

    def q_task(s, c):
        cos_q = cos_ref[slab_rows(s), :] * (HEAD_DIM ** -0.5 * LOG2_E)
        sin_q = sin_ref[slab_rows(s), :] * (HEAD_DIM ** -0.5 * LOG2_E)
        lo = c * QKV_COLS
        qc = jnp.dot(xn_ref[slab_rows(s), :], wqkv_ref[:, lo:lo + QKV_COLS],
                     preferred_element_type=F32) + bqkv_ref[mixer:mixer + 1, lo:lo + QKV_COLS]
        for j in range(QKV_COLS // V7X_LANES):
            col = qc[:, j * V7X_LANES:(j + 1) * V7X_LANES]
            q_ref[slab_rows(s), lo + j * V7X_LANES:lo + (j + 1) * V7X_LANES] = (
                _rope(col, cos_q, sin_q, is_low).astype(BF16))

    def kv_task(s):
        cos_k = cos_ref[slab_rows(s), :]
        sin_k = sin_ref[slab_rows(s), :]
        kvc = (jnp.dot(xn_ref[slab_rows(s), :], wqkv_ref[:, Q_WIDTH:Q_WIDTH + 2 * KV_WIDTH],
                       preferred_element_type=F32)
               + bqkv_ref[mixer:mixer + 1, Q_WIDTH:Q_WIDTH + 2 * KV_WIDTH])
        for j in range(KV_WIDTH // V7X_LANES):
            kj = _rope(kvc[:, j * V7X_LANES:(j + 1) * V7X_LANES], cos_k, sin_k, is_low)
            swapped = pltpu.roll(kj, HEAD_DIM, axis=1)
            even = jnp.where(first_head, kj, swapped)
            odd = jnp.where(first_head, swapped, kj)
            kd_ref[kv_rows(s), (2 * j) * V7X_LANES:(2 * j + 1) * V7X_LANES] = even.astype(BF16)
            kd_ref[kv_rows(s), (2 * j + 1) * V7X_LANES:(2 * j + 2) * V7X_LANES] = odd.astype(BF16)
        vt_ref[:, kv_rows(s)] = kvc[:, KV_WIDTH:2 * KV_WIDTH].T.astype(BF16)

    def out_task(s):
        att = ot_ref[:, slab_rows(s)].T.astype(BF16)
        mix = jnp.dot(att, wo_ref[...], preferred_element_type=F32)
        o_ref[slab_rows(s), :] = h_ref[slab_rows(s), :] + _rmsnorm(mix, post_ref[layer:layer + 1, :])

    cols4 = GQA_GROUP * Q_SUB
    sj = lax.broadcasted_iota(jnp.int32, (SUB_KEYS, cols4), 0)
    qi = lax.broadcasted_iota(jnp.int32, (SUB_KEYS, cols4), 1) % Q_SUB
    diff = ATTN_BLOCK + qi - sj
    band = (diff >= 0) & (diff < WINDOW)
    no_prev = seq_tile == 0
    band_first = [band & (sj >= jnp.where(no_prev, ATTN_BLOCK - u * Q_SUB, 0))
                  for u in range(ATTN_BLOCK // Q_SUB)]
    half = lax.broadcasted_iota(jnp.int32, (Q_SUB, V7X_LANES), 1) < HEAD_DIM
    zero_q = jnp.zeros((Q_SUB, V7X_LANES), BF16)
    head_of_lane = lax.broadcasted_iota(jnp.int32, (1, cols4), 1) // Q_SUB
    ones_rows = jnp.ones((SUM_ROWS, 2 * ATTN_BLOCK), BF16)
    zero_keys = jnp.zeros((2 * ATTN_BLOCK - SUB_KEYS, cols4), BF16)

    sinks = []
    for kv in range(N_KV_HEADS):
        sink = jnp.zeros((1, cols4), F32)
        for e in range(GQA_GROUP):
            sink = jnp.where(head_of_lane == e, sinks_ref[mixer, GQA_GROUP * kv + e], sink)
        sinks.append(sink * LOG2_E)

    def scores(kv, u):
        rows = slice(u * Q_SUB, (u + 1) * Q_SUB)
        keys = slice(u * Q_SUB, u * Q_SUB + SUB_KEYS)
        stacked = []
        for pair in range(GQA_GROUP // 2):
            qcols = slice((2 * kv + pair) * V7X_LANES, (2 * kv + pair + 1) * V7X_LANES)
            qp = q_ref[rows, qcols]
            stacked.append(jnp.where(half, qp, zero_q))
            stacked.append(jnp.where(half, zero_q, qp))
        q4 = jnp.concatenate(stacked, axis=0)
        return lax.dot_general(kd_ref[keys, kv * V7X_LANES:(kv + 1) * V7X_LANES], q4,
                               (((1,), (1,)), ((), ())), preferred_element_type=F32)

    def finish(kv, u, s):
        block = u // (ATTN_BLOCK // Q_SUB)
        part = u % (ATTN_BLOCK // Q_SUB)
        rows = slice(u * Q_SUB, (u + 1) * Q_SUB)
        window = slice(block * ATTN_BLOCK, (block + 2) * ATTN_BLOCK)
        valid = band_first[u] if block == 0 else band
        sink = sinks[kv]
        s = jnp.where(valid, s, NEG_INF)
        m = jnp.maximum(jnp.max(s, axis=0, keepdims=True), sink)
        p = jnp.exp2(s - m).astype(BF16)
        p_win = jnp.concatenate([p, zero_keys] if part == 0 else [zero_keys, p], axis=0)
        v_aug = jnp.concatenate(
            [vt_ref[kv * HEAD_DIM:(kv + 1) * HEAD_DIM, window], ones_rows], axis=0)
        ot = jnp.dot(v_aug, p_win, preferred_element_type=F32)
        denom = ot[HEAD_DIM:HEAD_DIM + 1, :] + jnp.exp2(sink - m)
        ot = ot[0:HEAD_DIM, :] * (1.0 / denom)
        for e in range(GQA_GROUP):
            head = GQA_GROUP * kv + e
            ot_ref[head * HEAD_DIM:(head + 1) * HEAD_DIM, rows] = (
                ot[:, e * Q_SUB:(e + 1) * Q_SUB])

    subs = ROW_SLAB // Q_SUB
    order = []
    for s in range(n_slabs):
        units = [(kv, u) for kv in range(N_KV_HEADS) for u in range(s * subs, (s + 1) * subs)]
        tasks = []
        if s + 1 < n_slabs:
            tasks += [functools.partial(norm_task, s + 1)]
            tasks += [functools.partial(q_task, s + 1, c) for c in range(Q_WIDTH // QKV_COLS)]
            tasks += [functools.partial(kv_task, s + 1)]
        if s > 0:
            tasks += [functools.partial(out_task, s - 1)]
        span = len(units) - SCORE_LOOKAHEAD
        after = {}
        for t, task in enumerate(tasks):
            after.setdefault(max((t + 1) * span // (len(tasks) + 1) - 1, 0), []).append(task)
        for i, unit in enumerate(units):
            order.append(unit)
            order.extend(after.get(i, []))
    unit_list = [e for e in order if isinstance(e, tuple)]

    norm_task(0)
    for c in range(Q_WIDTH // QKV_COLS):
        q_task(0, c)
    kv_task(0)
    scored = []
    done = 0
    for entry in order:
        if not isinstance(entry, tuple):
            entry()
            continue
        ahead = min(done + 1 + SCORE_LOOKAHEAD, len(unit_list))
        while done + len(scored) < ahead:
            scored.append(scores(*unit_list[done + len(scored)]))
        finish(*entry, scored.pop(0))
        done += 1
    out_task(n_slabs - 1)

    kd_ref[0:ATTN_BLOCK, :] = kd_ref[tm:tm + ATTN_BLOCK, :]
    vt_ref[:, 0:ATTN_BLOCK] = vt_ref[:, tm:tm + ATTN_BLOCK]


def _attn_layer(layer, mixer, h, rope_tab, pre_g, post_g, w_qkv, b_qkv, sinks, w_o,
                batch, seq, casts):
    tokens = h.shape[0]
    tm = ATTN_TILE
    tiles = seq // tm
    step_of = lambda b, i: b * tiles + i
    jobs = [_CastJob(stack, idx, batch * tiles) for stack, idx in casts]
    buffers = (4 * tm * D_MODEL * 4 + 4 * tm * V7X_LANES * 4
               + w_qkv.size * 2 + w_o.size * 2 + 2 * tm * D_MODEL * 2
               + (tm + ATTN_BLOCK) * (KV_DUP_WIDTH + KV_WIDTH) * 2
               + tm * Q_WIDTH * 4 + sum(job.vmem_bytes() for job in jobs))
    tile_map = lambda b, i: (b * tiles + i, 0)
    out = pl.pallas_call(
        functools.partial(_attn_body, layer, mixer, jobs),
        out_shape=[jax.ShapeDtypeStruct((tokens, D_MODEL), F32)]
        + [job.out_shape() for job in jobs],
        grid=(batch, tiles),
        in_specs=[
            pl.BlockSpec(memory_space=pltpu.SMEM),
            pl.BlockSpec((tm, D_MODEL), tile_map),
            pl.BlockSpec((tm, V7X_LANES), tile_map),
            _resident(pre_g.shape),
            _resident(post_g.shape),
            _resident(w_qkv.shape),
            _resident(b_qkv.shape),
            _resident(w_o.shape),
        ] + [job.in_spec(step_of) for job in jobs],
        out_specs=[pl.BlockSpec((tm, D_MODEL), tile_map)]
        + [job.out_spec(step_of) for job in jobs],
        scratch_shapes=[
            pltpu.VMEM((tm, D_MODEL), BF16),
            pltpu.VMEM((tm, Q_WIDTH), BF16),
            pltpu.VMEM((tm + ATTN_BLOCK, KV_DUP_WIDTH), BF16),
            pltpu.VMEM((KV_WIDTH, tm + ATTN_BLOCK), BF16),
            pltpu.VMEM((Q_WIDTH, tm), F32),
            pltpu.VMEM((tm, V7X_LANES), F32),
            pltpu.VMEM((tm, V7X_LANES), F32),
        ],
        compiler_params=pltpu.CompilerParams(
            dimension_semantics=("arbitrary", "arbitrary"),
            vmem_limit_bytes=_vmem_limit(buffers)),
        name="swa",
    )(sinks, h, rope_tab, pre_g, post_g, w_qkv, b_qkv, w_o, *[job.stack for job in jobs])
    return out[0], out[1:]


def _rope_lane_table(positions):
    inv_freq = ROPE_THETA ** (-jnp.arange(0, ROPE_DIM, 2, dtype=F32) / ROPE_DIM)
    ang = positions.astype(F32).reshape(-1, 1) * inv_freq
    cos, sin = jnp.cos(ang), jnp.sin(ang)
    tokens = ang.shape[0]
    rest = HEAD_DIM - ROPE_DIM
    return jnp.concatenate([cos, cos, jnp.ones((tokens, rest), F32),
                            -sin, sin, jnp.zeros((tokens, rest), F32)], axis=1)


def kernel(x, positions, pre_mix_g, post_mix_g, pre_ffn_g, post_ffn_g,
           a_w_in, a_b_in, a_ln_g, a_ln_b, a_w_s, a_b_s, a_w_out,
           b_w_qkv, b_b_qkv, b_sinks, b_w_o, ffn_w_gu, ffn_w_down):
    batch, seq, d_model = x.shape
    depth = pre_mix_g.shape[0]
    assert d_model == D_MODEL
    assert seq % ATTN_TILE == 0 and seq % GMLP_TILE == 0 and (batch * seq) % FFN_TILE == 0
    rope_tab = _rope_lane_table(positions)
    a_b_s_t = jnp.swapaxes(a_b_s, 1, 2)

    calls = []
    for i in range(depth):
        j = i // 2
        if i % 2 == 0:
            calls.append(("gmlp", i, j, [(a_w_in, j), (a_w_out, j)]))
        else:
            calls.append(("swa", i, j, [(b_w_qkv, j), (b_w_o, j)]))
        calls.append(("ffn", i, i, [(ffn_w_gu, i), (ffn_w_down, i)]))

    ready = {k: [stack[idx].astype(BF16) for stack, idx in calls[k][3]]
             for k in range(min(CAST_AHEAD, len(calls)))}
    h = x.reshape(batch * seq, d_model)
    for k, (kind, i, j, _) in enumerate(calls):
        casts = calls[k + CAST_AHEAD][3] if k + CAST_AHEAD < len(calls) else []
        w_a, w_b = ready.pop(k)
        if kind == "gmlp":
            h, cast = _gmlp_layer(i, j, h, pre_mix_g, post_mix_g, w_a, a_b_in, a_ln_g, a_ln_b,
                                  a_w_s, a_b_s_t, w_b, casts)
        elif kind == "swa":
            h, cast = _attn_layer(i, j, h, rope_tab, pre_mix_g, post_mix_g, w_a, b_b_qkv,
                                  b_sinks, w_b, batch, seq, casts)
        else:
            h, cast = _ffn_layer(i, h, pre_ffn_g, post_ffn_g, w_a, w_b, casts)
        if casts:
            ready[k + CAST_AHEAD] = cast
    return h.reshape(batch, seq, d_model)
```

```python
import functools

import jax
import jax.numpy as jnp
from jax import lax
from jax.experimental import pallas as pl
from jax.experimental.pallas import tpu as pltpu

D_MODEL = 1024
CHUNK = 128
GMLP_WIDTH = 2 * D_MODEL
GMLP_GROUPS = 8
GMLP_GROUP_DIM = GMLP_WIDTH // GMLP_GROUPS
HEAD_DIM = 64
N_Q_HEADS = D_MODEL // HEAD_DIM
N_KV_HEADS = 4
GQA_GROUP = N_Q_HEADS // N_KV_HEADS
WINDOW = 128
ATTN_BLOCK = 128
ROPE_DIM = HEAD_DIM // 4
ROPE_THETA = 500000.0
Q_WIDTH = N_Q_HEADS * HEAD_DIM
KV_WIDTH = N_KV_HEADS * HEAD_DIM
FFN_HIDDEN = -(-(8 * D_MODEL) // (3 * 256)) * 256
RMS_EPS = 1e-6
LN_EPS = 1e-5
NEG_INF = -1e30

V7X_LANES = 128
V7X_VMEM_BYTES = 64 * 1024 * 1024

F32 = jnp.float32
BF16 = jnp.bfloat16

FFN_TILE = 1024
GMLP_TILE = 1024
ATTN_TILE = 1024
ROW_SLAB = 256
RING_SLABS = 2
GMLP_IN_COLS = 512
FFN_HIDDEN_COLS = 256


def _vmem_limit(buffer_bytes):
    return int(min(2 * buffer_bytes, V7X_VMEM_BYTES - 8 * 1024 * 1024))


def _rmsnorm(x, g):
    ms = jnp.mean(x * x, axis=-1, keepdims=True)
    return x * lax.rsqrt(ms + RMS_EPS) * g


def _resident(shape):
    zeros = (0,) * len(shape)
    return pl.BlockSpec(shape, lambda *_: zeros)


def _layer_resident(stack_shape, layer):
    index = (layer,) + (0,) * (len(stack_shape) - 1)
    return pl.BlockSpec((None,) + tuple(stack_shape[1:]), lambda *_: index,
                        pipeline_mode=pl.Buffered(1))


BF16_SUBLANES = 16
CAST_AHEAD = 1


class _CastJob:
    def __init__(self, stack, layer, steps):
        self.stack, self.layer = stack, layer
        _, self.rows, self.cols = stack.shape
        self.bands = steps
        while (self.rows % self.bands or (self.rows // self.bands) % BF16_SUBLANES):
            self.bands //= 2
        self.band_rows = self.rows // self.bands

    def in_spec(self, step_of):
        layer, last = self.layer, self.bands - 1
        return pl.BlockSpec((None, self.band_rows, self.cols),
                            lambda *ids: (layer, jnp.minimum(step_of(*ids), last), 0))

    def out_spec(self, step_of):
        last = self.bands - 1
        return pl.BlockSpec((self.band_rows, self.cols),
                            lambda *ids: (jnp.minimum(step_of(*ids), last), 0))

    def out_shape(self):
        return jax.ShapeDtypeStruct((self.rows, self.cols), BF16)

    def vmem_bytes(self):
        return 2 * self.band_rows * self.cols * (4 + 2)


def _cast_bands(step, jobs, src_refs, dst_refs):
    for job, src_ref, dst_ref in zip(jobs, src_refs, dst_refs):
        @pl.when(step < job.bands)
        def _(src_ref=src_ref, dst_ref=dst_ref):
            dst_ref[...] = src_ref[...].astype(BF16)


def _split_refs(refs, n_cast):
    return (refs[:n_cast], refs[n_cast], refs[n_cast + 1:2 * n_cast + 1],
            refs[2 * n_cast + 1:])


def _ffn_body(layer, jobs, h_ref, pre_ref, post_ref, wgu_ref, wd_ref, *refs):
    cast_src, o_ref, cast_dst, (xn_ref, act_ref) = _split_refs(refs, len(jobs))
    _cast_bands(pl.program_id(0), jobs, cast_src, cast_dst)
    n_slabs = h_ref.shape[0] // ROW_SLAB

    def rows(s):
        return slice(s * ROW_SLAB, (s + 1) * ROW_SLAB)

    def prologue(s):
        xn_ref[rows(s), :] = _rmsnorm(h_ref[rows(s), :], pre_ref[layer:layer + 1, :]).astype(BF16)

    def gate_up(s):
        xn = xn_ref[rows(s), :]
        for c in range(FFN_HIDDEN // FFN_HIDDEN_COLS):
            lo = c * FFN_HIDDEN_COLS
            gate = jnp.dot(xn, wgu_ref[:, lo:lo + FFN_HIDDEN_COLS],
                           preferred_element_type=F32)
            up = jnp.dot(xn, wgu_ref[:, FFN_HIDDEN + lo:FFN_HIDDEN + lo + FFN_HIDDEN_COLS],
                         preferred_element_type=F32)
            act = gate * (1.0 / (1.0 + jnp.exp(-gate))) * up
            act_ref[rows(s), lo:lo + FFN_HIDDEN_COLS] = act.astype(BF16)

    def down(s):
        f = jnp.dot(act_ref[rows(s), :], wd_ref[...], preferred_element_type=F32)
        o_ref[rows(s), :] = h_ref[rows(s), :] + _rmsnorm(f, post_ref[layer:layer + 1, :])

    prologue(0)
    for s in range(n_slabs):
        if s + 1 < n_slabs:
            prologue(s + 1)
        gate_up(s)
        if s > 0:
            down(s - 1)
    down(n_slabs - 1)


def _ffn_layer(layer, h, pre_g, post_g, w_gu, w_down, casts):
    tokens = h.shape[0]
    tm = FFN_TILE
    steps = tokens // tm
    step_of = lambda i: i
    jobs = [_CastJob(stack, idx, steps) for stack, idx in casts]
    buffers = (4 * tm * D_MODEL * 4 + w_gu.size * 2 + w_down.size * 2
               + tm * D_MODEL * 2 + tm * FFN_HIDDEN * 2
               + sum(job.vmem_bytes() for job in jobs))
    out = pl.pallas_call(
        functools.partial(_ffn_body, layer, jobs),
        out_shape=[jax.ShapeDtypeStruct((tokens, D_MODEL), F32)]
        + [job.out_shape() for job in jobs],
        grid=(steps,),
        in_specs=[
            pl.BlockSpec((tm, D_MODEL), lambda i: (i, 0)),
            _resident(pre_g.shape),
            _resident(post_g.shape),
            _resident(w_gu.shape),
            _resident(w_down.shape),
        ] + [job.in_spec(step_of) for job in jobs],
        out_specs=[pl.BlockSpec((tm, D_MODEL), lambda i: (i, 0))]
        + [job.out_spec(step_of) for job in jobs],
        scratch_shapes=[
            pltpu.VMEM((tm, D_MODEL), BF16),
            pltpu.VMEM((tm, FFN_HIDDEN), BF16),
        ],
        compiler_params=pltpu.CompilerParams(
            dimension_semantics=("arbitrary",),
            vmem_limit_bytes=_vmem_limit(buffers)),
        name="ffn",
    )(h, pre_g, post_g, w_gu, w_down, *[job.stack for job in jobs])
    return out[0], out[1:]


def _gmlp_body(layer, mixer, jobs, h_ref, pre_ref, post_ref, win_ref, bin_ref, lng_ref,
               lnb_ref, ws_ref, bst_ref, wout_ref, *refs):
    cast_src, o_ref, cast_dst, (xn_ref, u_ref, v_ref, gated_ref) = _split_refs(refs, len(jobs))
    _cast_bands(pl.program_id(0), jobs, cast_src, cast_dst)
    n_slabs = h_ref.shape[0] // ROW_SLAB

    def rows(s):
        return slice(s * ROW_SLAB, (s + 1) * ROW_SLAB)

    def ring(s):
        return rows(s % RING_SLABS)

    def prologue(s):
        xn_ref[rows(s), :] = _rmsnorm(h_ref[rows(s), :], pre_ref[layer:layer + 1, :]).astype(BF16)

    def project_in(s):
        xn = xn_ref[rows(s), :]
        for c in range(2 * GMLP_WIDTH // GMLP_IN_COLS):
            lo = c * GMLP_IN_COLS
            z = jnp.dot(xn, win_ref[:, lo:lo + GMLP_IN_COLS],
                        preferred_element_type=F32) + bin_ref[mixer:mixer + 1, lo:lo + GMLP_IN_COLS]
            z2 = z * (1.0 + lax.erf(z * 0.7071067811865476))
            if lo < GMLP_WIDTH:
                u_ref[ring(s), lo:lo + GMLP_IN_COLS] = z2
            else:
                v_ref[ring(s), lo - GMLP_WIDTH:lo - GMLP_WIDTH + GMLP_IN_COLS] = z2

    row = lax.broadcasted_iota(jnp.int32, (CHUNK, CHUNK), 0)
    col = lax.broadcasted_iota(jnp.int32, (CHUNK, CHUNK), 1)
    causal = col <= row
    half_w_spatial = [jnp.where(causal, 0.5 * ws_ref[g], 0.0).astype(BF16)
                      for g in range(GMLP_GROUPS)]
    half_b_spatial = 0.5 * bst_ref[...]

    def spatial_gate(s):
        base = (s % RING_SLABS) * ROW_SLAB
        for r in range(ROW_SLAB // CHUNK):
            chunk = slice(base + r * CHUNK, base + (r + 1) * CHUNK)
            v2 = v_ref[chunk, :]
            mu = jnp.mean(v2, axis=-1, keepdims=True)
            d = v2 - mu
            var = jnp.mean(d * d, axis=-1, keepdims=True)
            vn = (d * lax.rsqrt(var + 4.0 * LN_EPS) * lng_ref[mixer:mixer + 1, :]
                  + lnb_ref[mixer:mixer + 1, :]).astype(BF16)
            for g in range(GMLP_GROUPS):
                cols = slice(g * GMLP_GROUP_DIM, (g + 1) * GMLP_GROUP_DIM)
                half_sv = jnp.dot(half_w_spatial[g], vn[:, cols], preferred_element_type=F32)
                half_sv = half_sv + half_b_spatial[:, g:g + 1]
                gated_ref[chunk, cols] = (u_ref[chunk, cols] * half_sv).astype(BF16)

    def project_out(s):
        mix = jnp.dot(gated_ref[ring(s), :], wout_ref[...], preferred_element_type=F32)
        o_ref[rows(s), :] = h_ref[rows(s), :] + _rmsnorm(mix, post_ref[layer:layer + 1, :])

    prologue(0)
    project_in(0)
    for s in range(n_slabs):
        if s + 1 < n_slabs:
            prologue(s + 1)
            project_in(s + 1)
        spatial_gate(s)
        project_out(s)


def _gmlp_layer(layer, mixer, h, pre_g, post_g, w_in, b_in, ln_g, ln_b, w_s, b_s_t, w_out,
                casts):
    tokens = h.shape[0]
    tm = GMLP_TILE
    steps = tokens // tm
    step_of = lambda i: i
    jobs = [_CastJob(stack, idx, steps) for stack, idx in casts]
    ring_rows = RING_SLABS * ROW_SLAB
    buffers = (4 * tm * D_MODEL * 4 + w_in.size * 2 + w_out.size * 2
               + tm * D_MODEL * 2 + 2 * ring_rows * GMLP_WIDTH * 4
               + ring_rows * GMLP_WIDTH * 2 + sum(job.vmem_bytes() for job in jobs))
    out = pl.pallas_call(
        functools.partial(_gmlp_body, layer, mixer, jobs),
        out_shape=[jax.ShapeDtypeStruct((tokens, D_MODEL), F32)]
        + [job.out_shape() for job in jobs],
        grid=(steps,),
        in_specs=[
            pl.BlockSpec((tm, D_MODEL), lambda i: (i, 0)),
            _resident(pre_g.shape),
            _resident(post_g.shape),
            _resident(w_in.shape),
            _resident(b_in.shape),
            _resident(ln_g.shape),
            _resident(ln_b.shape),
            _layer_resident(w_s.shape, mixer),
            _layer_resident(b_s_t.shape, mixer),
            _resident(w_out.shape),
        ] + [job.in_spec(step_of) for job in jobs],
        out_specs=[pl.BlockSpec((tm, D_MODEL), lambda i: (i, 0))]
        + [job.out_spec(step_of) for job in jobs],
        scratch_shapes=[
            pltpu.VMEM((tm, D_MODEL), BF16),
            pltpu.VMEM((ring_rows, GMLP_WIDTH), F32),
            pltpu.VMEM((ring_rows, GMLP_WIDTH), F32),
            pltpu.VMEM((ring_rows, GMLP_WIDTH), BF16),
        ],
        compiler_params=pltpu.CompilerParams(
            dimension_semantics=("arbitrary",),
            vmem_limit_bytes=_vmem_limit(buffers)),
        name="gmlp",
    )(h, pre_g, post_g, w_in, b_in, ln_g, ln_b, w_s, b_s_t, w_out,
      *[job.stack for job in jobs])
    return out[0], out[1:]


KV_DUP_WIDTH = 2 * KV_WIDTH
QKV_COLS = 512
Q_SUB = 64
SUB_KEYS = Q_SUB + WINDOW
SUM_ROWS = 16
LOG2_E = 1.4426950408889634
SCORE_LOOKAHEAD = 4


def _rope(x, cos_tab, sin_tab, is_low):
    partner = jnp.where(is_low,
                        pltpu.roll(x, V7X_LANES - ROPE_DIM // 2, axis=1),
                        pltpu.roll(x, ROPE_DIM // 2, axis=1))
    return x * cos_tab + partner * sin_tab


def _attn_body(layer, mixer, jobs, sinks_ref, h_ref, pos_ref, invf_ref, pre_ref, post_ref,
               wqkv_ref, bqkv_ref, wo_ref, *refs):
    cast_src, o_ref, cast_dst, scratch = _split_refs(refs, len(jobs))
    xn_ref, q_ref, kd_ref, vt_ref, ot_ref, cos_ref, sin_ref = scratch
    _cast_bands(pl.program_id(0) * pl.num_programs(1) + pl.program_id(1),
                jobs, cast_src, cast_dst)
    tm = h_ref.shape[0]
    n_slabs = tm // ROW_SLAB
    seq_tile = pl.program_id(1)

    @pl.when(seq_tile == 0)
    def _():
        kd_ref[0:ATTN_BLOCK, :] = jnp.zeros((ATTN_BLOCK, KV_DUP_WIDTH), BF16)
        vt_ref[:, 0:ATTN_BLOCK] = jnp.zeros((KV_WIDTH, ATTN_BLOCK), BF16)

    lane = lax.broadcasted_iota(jnp.int32, (ROW_SLAB, V7X_LANES), 1)
    first_head = lane < HEAD_DIM
    is_low = (lane % HEAD_DIM) < (ROPE_DIM // 2)

    def slab_rows(s):
        return slice(s * ROW_SLAB, (s + 1) * ROW_SLAB)

    def kv_rows(s):
        return slice(ATTN_BLOCK + s * ROW_SLAB, ATTN_BLOCK + (s + 1) * ROW_SLAB)

    def norm_task(s):
        xn_ref[slab_rows(s), :] = _rmsnorm(h_ref[slab_rows(s), :],
                                           pre_ref[layer:layer + 1, :]).astype(BF16)
        ang = pos_ref[:, slab_rows(s)].astype(F32) * invf_ref[...]
        cos, sin = jnp.cos(ang), jnp.sin(ang)
        rest = (HEAD_DIM - ROPE_DIM, ROW_SLAB)
        cos_head = [cos, cos, jnp.ones(rest, F32)]
        sin_head = [-sin, sin, jnp.zeros(rest, F32)]
        reps = V7X_LANES // HEAD_DIM
        cos_ref[slab_rows(s), :] = jnp.concatenate(cos_head * reps, axis=0).T
        sin_ref[slab_rows(s), :] = jnp.concatenate(sin_head * reps, axis=0).T

    def q_task(s, c):
        cos_q = cos_ref[slab_rows(s), :] * (HEAD_DIM ** -0.5 * LOG2_E)
        sin_q = sin_ref[slab_rows(s), :] * (HEAD_DIM ** -0.5 * LOG2_E)
        lo = c * QKV_COLS
        qc = jnp.dot(xn_ref[slab_rows(s), :], wqkv_ref[:, lo:lo + QKV_COLS],
                     preferred_element_type=F32) + bqkv_ref[mixer:mixer + 1, lo:lo + QKV_COLS]
        for j in range(QKV_COLS // V7X_LANES):
            col = qc[:, j * V7X_LANES:(j + 1) * V7X_LANES]
            q_ref[slab_rows(s), lo + j * V7X_LANES:lo + (j + 1) * V7X_LANES] = (
                _rope(col, cos_q, sin_q, is_low).astype(BF16))

    def kv_task(s):
        cos_k = cos_ref[slab_rows(s), :]
        sin_k = sin_ref[slab_rows(s), :]
        kvc = (jnp.dot(xn_ref[slab_rows(s), :], wqkv_ref[:, Q_WIDTH:Q_WIDTH + 2 * KV_WIDTH],
                       preferred_element_type=F32)
               + bqkv_ref[mixer:mixer + 1, Q_WIDTH:Q_WIDTH + 2 * KV_WIDTH])
        for j in range(KV_WIDTH // V7X_LANES):
            kj = _rope(kvc[:, j * V7X_LANES:(j + 1) * V7X_LANES], cos_k, sin_k, is_low)
            swapped = pltpu.roll(kj, HEAD_DIM, axis=1)
            even = jnp.where(first_head, kj, swapped)
            odd = jnp.where(first_head, swapped, kj)
            kd_ref[kv_rows(s), (2 * j) * V7X_LANES:(2 * j + 1) * V7X_LANES] = even.astype(BF16)
            kd_ref[kv_rows(s), (2 * j + 1) * V7X_LANES:(2 * j + 2) * V7X_LANES] = odd.astype(BF16)
        vt_ref[:, kv_rows(s)] = kvc[:, KV_WIDTH:2 * KV_WIDTH].T.astype(BF16)

    def out_task(s):
        att = ot_ref[:, slab_rows(s)].T.astype(BF16)
        mix = jnp.dot(att, wo_ref[...], preferred_element_type=F32)
        o_ref[slab_rows(s), :] = h_ref[slab_rows(s), :] + _rmsnorm(mix, post_ref[layer:layer + 1, :])

    cols4 = GQA_GROUP * Q_SUB
    sj = lax.broadcasted_iota(jnp.int32, (SUB_KEYS, cols4), 0)
    qi = lax.broadcasted_iota(jnp.int32, (SUB_KEYS, cols4), 1) % Q_SUB
    diff = ATTN_BLOCK + qi - sj
    band = (diff >= 0) & (diff < WINDOW)
    no_prev = seq_tile == 0
    band_first = [band & (sj >= jnp.where(no_prev, ATTN_BLOCK - u * Q_SUB, 0))
                  for u in range(ATTN_BLOCK // Q_SUB)]
    half = lax.broadcasted_iota(jnp.int32, (Q_SUB, V7X_LANES), 1) < HEAD_DIM
    zero_q = jnp.zeros((Q_SUB, V7X_LANES), BF16)
    head_of_lane = lax.broadcasted_iota(jnp.int32, (1, cols4), 1) // Q_SUB
    ones_rows = jnp.ones((SUM_ROWS, 2 * ATTN_BLOCK), BF16)
    zero_keys = jnp.zeros((2 * ATTN_BLOCK - SUB_KEYS, cols4), BF16)

    sinks = []
    for kv in range(N_KV_HEADS):
        sink = jnp.zeros((1, cols4), F32)
        for e in range(GQA_GROUP):
            sink = jnp.where(head_of_lane == e, sinks_ref[mixer, GQA_GROUP * kv + e], sink)
        sinks.append(sink * LOG2_E)

    def scores(kv, u):
        rows = slice(u * Q_SUB, (u + 1) * Q_SUB)
        keys = slice(u * Q_SUB, u * Q_SUB + SUB_KEYS)
        stacked = []
        for pair in range(GQA_GROUP // 2):
            qcols = slice((2 * kv + pair) * V7X_LANES, (2 * kv + pair + 1) * V7X_LANES)
            qp = q_ref[rows, qcols]
            stacked.append(jnp.where(half, qp, zero_q))
            stacked.append(jnp.where(half, zero_q, qp))
        q4 = jnp.concatenate(stacked, axis=0)
        return lax.dot_general(kd_ref[keys, kv * V7X_LANES:(kv + 1) * V7X_LANES], q4,
                               (((1,), (1,)), ((), ())), preferred_element_type=F32)

    def finish(kv, u, s):
        block = u // (ATTN_BLOCK // Q_SUB)
        part = u % (ATTN_BLOCK // Q_SUB)
        rows = slice(u * Q_SUB, (u + 1) * Q_SUB)
        window = slice(block * ATTN_BLOCK, (block + 2) * ATTN_BLOCK)
        valid = band_first[u] if block == 0 else band
        sink = sinks[kv]
        s = jnp.where(valid, s, NEG_INF)
        m = jnp.maximum(jnp.max(s, axis=0, keepdims=True), sink)
        p = jnp.exp2(s - m).astype(BF16)
        p_win = jnp.concatenate([p, zero_keys] if part == 0 else [zero_keys, p], axis=0)
        v_aug = jnp.concatenate(
            [vt_ref[kv * HEAD_DIM:(kv + 1) * HEAD_DIM, window], ones_rows], axis=0)
        ot = jnp.dot(v_aug, p_win, preferred_element_type=F32)
        denom = ot[HEAD_DIM:HEAD_DIM + 1, :] + jnp.exp2(sink - m)
        ot = ot[0:HEAD_DIM, :] * (1.0 / denom)
        for e in range(GQA_GROUP):
            head = GQA_GROUP * kv + e
            ot_ref[head * HEAD_DIM:(head + 1) * HEAD_DIM, rows] = (
                ot[:, e * Q_SUB:(e + 1) * Q_SUB])

    subs = ROW_SLAB // Q_SUB
    order = []
    for s in range(n_slabs):
        units = [(kv, u) for kv in range(N_KV_HEADS) for u in range(s * subs, (s + 1) * subs)]
        tasks = []
        if s + 1 < n_slabs:
            tasks += [functools.partial(norm_task, s + 1)]
            tasks += [functools.partial(q_task, s + 1, c) for c in range(Q_WIDTH // QKV_COLS)]
            tasks += [functools.partial(kv_task, s + 1)]
        if s > 0:
            tasks += [functools.partial(out_task, s - 1)]
        span = len(units) - SCORE_LOOKAHEAD
        after = {}
        for t, task in enumerate(tasks):
            after.setdefault(max((t + 1) * span // (len(tasks) + 1) - 1, 0), []).append(task)
        for i, unit in enumerate(units):
            order.append(unit)
            order.extend(after.get(i, []))
    unit_list = [e for e in order if isinstance(e, tuple)]

    norm_task(0)
    for c in range(Q_WIDTH // QKV_COLS):
        q_task(0, c)
    kv_task(0)
    scored = []
    done = 0
    for entry in order:
        if not isinstance(entry, tuple):
            entry()
            continue
        ahead = min(done + 1 + SCORE_LOOKAHEAD, len(unit_list))
        while done + len(scored) < ahead:
            scored.append(scores(*unit_list[done + len(scored)]))
        finish(*entry, scored.pop(0))
        done += 1
    out_task(n_slabs - 1)

    kd_ref[0:ATTN_BLOCK, :] = kd_ref[tm:tm + ATTN_BLOCK, :]
    vt_ref[:, 0:ATTN_BLOCK] = vt_ref[:, tm:tm + ATTN_BLOCK]


def _attn_layer(layer, mixer, h, positions, inv_freq, pre_g, post_g, w_qkv, b_qkv, sinks, w_o,
                batch, seq, casts):
    tokens = h.shape[0]
    tm = ATTN_TILE
    tiles = seq // tm
    step_of = lambda b, i: b * tiles + i
    jobs = [_CastJob(stack, idx, batch * tiles) for stack, idx in casts]
    buffers = (4 * tm * D_MODEL * 4 + 2 * tm * V7X_LANES * 4
               + w_qkv.size * 2 + w_o.size * 2 + 2 * tm * D_MODEL * 2
               + (tm + ATTN_BLOCK) * (KV_DUP_WIDTH + KV_WIDTH) * 2
               + tm * Q_WIDTH * 4 + sum(job.vmem_bytes() for job in jobs))
    tile_map = lambda b, i: (b * tiles + i, 0)
    out = pl.pallas_call(
        functools.partial(_attn_body, layer, mixer, jobs),
        out_shape=[jax.ShapeDtypeStruct((tokens, D_MODEL), F32)]
        + [job.out_shape() for job in jobs],
        grid=(batch, tiles),
        in_specs=[
            pl.BlockSpec(memory_space=pltpu.SMEM),
            pl.BlockSpec((tm, D_MODEL), tile_map),
            pl.BlockSpec((None, 1, tm), lambda b, i: (b, 0, i)),
            _resident(inv_freq.shape),
            _resident(pre_g.shape),
            _resident(post_g.shape),
            _resident(w_qkv.shape),
            _resident(b_qkv.shape),
            _resident(w_o.shape),
        ] + [job.in_spec(step_of) for job in jobs],
        out_specs=[pl.BlockSpec((tm, D_MODEL), tile_map)]
        + [job.out_spec(step_of) for job in jobs],
        scratch_shapes=[
            pltpu.VMEM((tm, D_MODEL), BF16),
            pltpu.VMEM((tm, Q_WIDTH), BF16),
            pltpu.VMEM((tm + ATTN_BLOCK, KV_DUP_WIDTH), BF16),
            pltpu.VMEM((KV_WIDTH, tm + ATTN_BLOCK), BF16),
            pltpu.VMEM((Q_WIDTH, tm), F32),
            pltpu.VMEM((tm, V7X_LANES), F32),
            pltpu.VMEM((tm, V7X_LANES), F32),
        ],
        compiler_params=pltpu.CompilerParams(
            dimension_semantics=("arbitrary", "arbitrary"),
            vmem_limit_bytes=_vmem_limit(buffers)),
        name="swa",
    )(sinks, h, positions.reshape(batch, 1, seq), inv_freq, pre_g, post_g, w_qkv, b_qkv, w_o,
      *[job.stack for job in jobs])
    return out[0], out[1:]


def kernel(x, positions, pre_mix_g, post_mix_g, pre_ffn_g, post_ffn_g,
           a_w_in, a_b_in, a_ln_g, a_ln_b, a_w_s, a_b_s, a_w_out,
           b_w_qkv, b_b_qkv, b_sinks, b_w_o, ffn_w_gu, ffn_w_down):
    batch, seq, d_model = x.shape
    depth = pre_mix_g.shape[0]
    assert d_model == D_MODEL
    assert seq % ATTN_TILE == 0 and seq % GMLP_TILE == 0 and (batch * seq) % FFN_TILE == 0
    inv_freq = (ROPE_THETA ** (-jnp.arange(0, ROPE_DIM, 2, dtype=F32) / ROPE_DIM)).reshape(-1, 1)
    a_b_s_t = jnp.swapaxes(a_b_s, 1, 2)

    calls = []
    for i in range(depth):
        j = i // 2
        if i % 2 == 0:
            calls.append(("gmlp", i, j, [(a_w_in, j), (a_w_out, j)]))
        else:
            calls.append(("swa", i, j, [(b_w_qkv, j), (b_w_o, j)]))
        calls.append(("ffn", i, i, [(ffn_w_gu, i), (ffn_w_down, i)]))

    ready = {k: [stack[idx].astype(BF16) for stack, idx in calls[k][3]]
             for k in range(min(CAST_AHEAD, len(calls)))}
    h = x.reshape(batch * seq, d_model)
    for k, (kind, i, j, _) in enumerate(calls):
        casts = calls[k + CAST_AHEAD][3] if k + CAST_AHEAD < len(calls) else []
        w_a, w_b = ready.pop(k)
        if kind == "gmlp":
            h, cast = _gmlp_layer(i, j, h, pre_mix_g, post_mix_g, w_a, a_b_in, a_ln_g, a_ln_b,
                                  a_w_s, a_b_s_t, w_b, casts)
        elif kind == "swa":
            h, cast = _attn_layer(i, j, h, positions, inv_freq, pre_mix_g, post_mix_g, w_a,
                                  b_b_qkv, b_sinks, w_b, batch, seq, casts)
        else:
            h, cast = _ffn_layer(i, h, pre_ffn_g, post_ffn_g, w_a, w_b, casts)
        if casts:
            ready[k + CAST_AHEAD] = cast
    return h.reshape(batch, seq, d_model)
```

```python
import functools

import jax
import jax.numpy as jnp
from jax import lax
from jax.experimental import pallas as pl
from jax.experimental.pallas import tpu as pltpu

D_MODEL = 1024
CHUNK = 128
GMLP_WIDTH = 2 * D_MODEL
GMLP_GROUPS = 8
GMLP_GROUP_DIM = GMLP_WIDTH // GMLP_GROUPS
HEAD_DIM = 64
N_Q_HEADS = D_MODEL // HEAD_DIM
N_KV_HEADS = 4
GQA_GROUP = N_Q_HEADS // N_KV_HEADS
WINDOW = 128
ATTN_BLOCK = 128
ROPE_DIM = HEAD_DIM // 4
ROPE_THETA = 500000.0
Q_WIDTH = N_Q_HEADS * HEAD_DIM
KV_WIDTH = N_KV_HEADS * HEAD_DIM
FFN_HIDDEN = -(-(8 * D_MODEL) // (3 * 256)) * 256
RMS_EPS = 1e-6
LN_EPS = 1e-5
NEG_INF = -1e30

V7X_LANES = 128
V7X_VMEM_BYTES = 64 * 1024 * 1024

F32 = jnp.float32
BF16 = jnp.bfloat16

FFN_TILE = 1024
GMLP_TILE = 1024
ATTN_TILE = 1024
ROW_SLAB = 256
GMLP_SLAB = 256
RING_SLABS = 2
GMLP_IN_COLS = 512
FFN_HIDDEN_COLS = 256


def _vmem_limit(buffer_bytes):
    return int(min(2 * buffer_bytes, V7X_VMEM_BYTES - 8 * 1024 * 1024))


def _rmsnorm(x, g):
    ms = jnp.mean(x * x, axis=-1, keepdims=True)
    return x * lax.rsqrt(ms + RMS_EPS) * g


def _resident(shape):
    zeros = (0,) * len(shape)
    return pl.BlockSpec(shape, lambda *_: zeros)


def _layer_resident(stack_shape, layer):
    index = (layer,) + (0,) * (len(stack_shape) - 1)
    return pl.BlockSpec((None,) + tuple(stack_shape[1:]), lambda *_: index,
                        pipeline_mode=pl.Buffered(1))


BF16_SUBLANES = 16
CAST_AHEAD = 1


class _CastJob:
    def __init__(self, stack, layer, steps):
        self.stack, self.layer = stack, layer
        _, self.rows, self.cols = stack.shape
        self.bands = steps
        while (self.rows % self.bands or (self.rows // self.bands) % BF16_SUBLANES):
            self.bands //= 2
        self.band_rows = self.rows // self.bands

    def in_spec(self, step_of):
        layer, last = self.layer, self.bands - 1
        return pl.BlockSpec((None, self.band_rows, self.cols),
                            lambda *ids: (layer, jnp.minimum(step_of(*ids), last), 0))

    def out_spec(self, step_of):
        last = self.bands - 1
        return pl.BlockSpec((self.band_rows, self.cols),
                            lambda *ids: (jnp.minimum(step_of(*ids), last), 0))

    def out_shape(self):
        return jax.ShapeDtypeStruct((self.rows, self.cols), BF16)

    def vmem_bytes(self):
        return 2 * self.band_rows * self.cols * (4 + 2)


def _cast_bands(step, jobs, src_refs, dst_refs):
    for job, src_ref, dst_ref in zip(jobs, src_refs, dst_refs):
        @pl.when(step < job.bands)
        def _(src_ref=src_ref, dst_ref=dst_ref):
            dst_ref[...] = src_ref[...].astype(BF16)


def _split_refs(refs, n_cast):
    return (refs[:n_cast], refs[n_cast], refs[n_cast + 1:2 * n_cast + 1],
            refs[2 * n_cast + 1:])


def _ffn_body(layer, jobs, h_ref, pre_ref, post_ref, wgu_ref, wd_ref, *refs):
    cast_src, o_ref, cast_dst, (xn_ref, act_ref) = _split_refs(refs, len(jobs))
    _cast_bands(pl.program_id(0), jobs, cast_src, cast_dst)
    n_slabs = h_ref.shape[0] // ROW_SLAB

    def rows(s):
        return slice(s * ROW_SLAB, (s + 1) * ROW_SLAB)

    def prologue(s):
        xn_ref[rows(s), :] = _rmsnorm(h_ref[rows(s), :], pre_ref[layer:layer + 1, :]).astype(BF16)

    def gate_up(s):
        xn = xn_ref[rows(s), :]
        for c in range(FFN_HIDDEN // FFN_HIDDEN_COLS):
            lo = c * FFN_HIDDEN_COLS
            gate = jnp.dot(xn, wgu_ref[:, lo:lo + FFN_HIDDEN_COLS],
                           preferred_element_type=F32)
            up = jnp.dot(xn, wgu_ref[:, FFN_HIDDEN + lo:FFN_HIDDEN + lo + FFN_HIDDEN_COLS],
                         preferred_element_type=F32)
            act = gate * (1.0 / (1.0 + jnp.exp(-gate))) * up
            act_ref[rows(s), lo:lo + FFN_HIDDEN_COLS] = act.astype(BF16)

    def down(s):
        f = jnp.dot(act_ref[rows(s), :], wd_ref[...], preferred_element_type=F32)
        o_ref[rows(s), :] = h_ref[rows(s), :] + _rmsnorm(f, post_ref[layer:layer + 1, :])

    prologue(0)
    for s in range(n_slabs):
        if s + 1 < n_slabs:
            prologue(s + 1)
        gate_up(s)
        if s > 0:
            down(s - 1)
    down(n_slabs - 1)


def _ffn_layer(layer, h, pre_g, post_g, w_gu, w_down, casts):
    tokens = h.shape[0]
    tm = FFN_TILE
    steps = tokens // tm
    step_of = lambda i: i
    jobs = [_CastJob(stack, idx, steps) for stack, idx in casts]
    buffers = (4 * tm * D_MODEL * 4 + w_gu.size * 2 + w_down.size * 2
               + tm * D_MODEL * 2 + tm * FFN_HIDDEN * 2
               + sum(job.vmem_bytes() for job in jobs))
    out = pl.pallas_call(
        functools.partial(_ffn_body, layer, jobs),
        out_shape=[jax.ShapeDtypeStruct((tokens, D_MODEL), F32)]
        + [job.out_shape() for job in jobs],
        grid=(steps,),
        in_specs=[
            pl.BlockSpec((tm, D_MODEL), lambda i: (i, 0)),
            _resident(pre_g.shape),
            _resident(post_g.shape),
            _resident(w_gu.shape),
            _resident(w_down.shape),
        ] + [job.in_spec(step_of) for job in jobs],
        out_specs=[pl.BlockSpec((tm, D_MODEL), lambda i: (i, 0))]
        + [job.out_spec(step_of) for job in jobs],
        scratch_shapes=[
            pltpu.VMEM((tm, D_MODEL), BF16),
            pltpu.VMEM((tm, FFN_HIDDEN), BF16),
        ],
        compiler_params=pltpu.CompilerParams(
            dimension_semantics=("arbitrary",),
            vmem_limit_bytes=_vmem_limit(buffers)),
        name="ffn",
    )(h, pre_g, post_g, w_gu, w_down, *[job.stack for job in jobs])
    return out[0], out[1:]


def _gmlp_body(layer, mixer, jobs, h_ref, pre_ref, post_ref, win_ref, bin_ref, lng_ref,
               lnb_ref, ws_ref, bst_ref, wout_ref, *refs):
    cast_src, o_ref, cast_dst, (xn_ref, u_ref, v_ref, gated_ref) = _split_refs(refs, len(jobs))
    _cast_bands(pl.program_id(0), jobs, cast_src, cast_dst)
    n_slabs = h_ref.shape[0] // GMLP_SLAB

    def rows(s):
        return slice(s * GMLP_SLAB, (s + 1) * GMLP_SLAB)

    def ring(s):
        return rows(s % RING_SLABS)

    def prologue(s):
        xn_ref[rows(s), :] = _rmsnorm(h_ref[rows(s), :], pre_ref[layer:layer + 1, :]).astype(BF16)

    def project_in(s, chunks):
        xn = xn_ref[rows(s), :]
        for c in chunks:
            lo = c * GMLP_IN_COLS
            z = jnp.dot(xn, win_ref[:, lo:lo + GMLP_IN_COLS],
                        preferred_element_type=F32) + bin_ref[mixer:mixer + 1, lo:lo + GMLP_IN_COLS]
            z2 = z * (1.0 + lax.erf(z * 0.7071067811865476))
            if lo < GMLP_WIDTH:
                u_ref[ring(s), lo:lo + GMLP_IN_COLS] = z2
            else:
                v_ref[ring(s), lo - GMLP_WIDTH:lo - GMLP_WIDTH + GMLP_IN_COLS] = z2

    row = lax.broadcasted_iota(jnp.int32, (CHUNK, CHUNK), 0)
    col = lax.broadcasted_iota(jnp.int32, (CHUNK, CHUNK), 1)
    causal = col <= row
    half_w_spatial = [jnp.where(causal, 0.5 * ws_ref[g], 0.0).astype(BF16)
                      for g in range(GMLP_GROUPS)]
    half_b_spatial = 0.5 * bst_ref[...]

    def spatial_gate(s, chunks):
        base = (s % RING_SLABS) * GMLP_SLAB
        for r in chunks:
            chunk = slice(base + r * CHUNK, base + (r + 1) * CHUNK)
            v2 = v_ref[chunk, :]
            mu = jnp.mean(v2, axis=-1, keepdims=True)
            d = v2 - mu
            var = jnp.mean(d * d, axis=-1, keepdims=True)
            vn = (d * lax.rsqrt(var + 4.0 * LN_EPS) * lng_ref[mixer:mixer + 1, :]
                  + lnb_ref[mixer:mixer + 1, :]).astype(BF16)
            for g in range(GMLP_GROUPS):
                cols = slice(g * GMLP_GROUP_DIM, (g + 1) * GMLP_GROUP_DIM)
                half_sv = jnp.dot(half_w_spatial[g], vn[:, cols], preferred_element_type=F32)
                half_sv = half_sv + half_b_spatial[:, g:g + 1]
                gated_ref[chunk, cols] = (u_ref[chunk, cols] * half_sv).astype(BF16)

    def project_out(s):
        mix = jnp.dot(gated_ref[ring(s), :], wout_ref[...], preferred_element_type=F32)
        o_ref[rows(s), :] = h_ref[rows(s), :] + _rmsnorm(mix, post_ref[layer:layer + 1, :])

    n_in = 2 * GMLP_WIDTH // GMLP_IN_COLS
    in_order = list(range(n_in // 2, n_in)) + list(range(n_in // 2))
    n_chunks = GMLP_SLAB // CHUNK
    prologue(0)
    project_in(0, in_order)
    for s in range(n_slabs):
        if s + 1 < n_slabs:
            prologue(s + 1)
            project_in(s + 1, in_order)
        spatial_gate(s, range(n_chunks))
        project_out(s)


def _gmlp_layer(layer, mixer, h, pre_g, post_g, w_in, b_in, ln_g, ln_b, w_s, b_s_t, w_out,
                casts):
    tokens = h.shape[0]
    tm = GMLP_TILE
    steps = tokens // tm
    step_of = lambda i: i
    jobs = [_CastJob(stack, idx, steps) for stack, idx in casts]
    ring_rows = RING_SLABS * GMLP_SLAB
    buffers = (4 * tm * D_MODEL * 4 + w_in.size * 2 + w_out.size * 2
               + tm * D_MODEL * 2 + 2 * ring_rows * GMLP_WIDTH * 4
               + ring_rows * GMLP_WIDTH * 2 + sum(job.vmem_bytes() for job in jobs))
    out = pl.pallas_call(
        functools.partial(_gmlp_body, layer, mixer, jobs),
        out_shape=[jax.ShapeDtypeStruct((tokens, D_MODEL), F32)]
        + [job.out_shape() for job in jobs],
        grid=(steps,),
        in_specs=[
            pl.BlockSpec((tm, D_MODEL), lambda i: (i, 0)),
            _resident(pre_g.shape),
            _resident(post_g.shape),
            _resident(w_in.shape),
            _resident(b_in.shape),
            _resident(ln_g.shape),
            _resident(ln_b.shape),
            _layer_resident(w_s.shape, mixer),
            _layer_resident(b_s_t.shape, mixer),
            _resident(w_out.shape),
        ] + [job.in_spec(step_of) for job in jobs],
        out_specs=[pl.BlockSpec((tm, D_MODEL), lambda i: (i, 0))]
        + [job.out_spec(step_of) for job in jobs],
        scratch_shapes=[
            pltpu.VMEM((tm, D_MODEL), BF16),
            pltpu.VMEM((ring_rows, GMLP_WIDTH), F32),
            pltpu.VMEM((ring_rows, GMLP_WIDTH), F32),
            pltpu.VMEM((ring_rows, GMLP_WIDTH), BF16),
        ],
        compiler_params=pltpu.CompilerParams(
            dimension_semantics=("arbitrary",),
            vmem_limit_bytes=_vmem_limit(buffers)),
        name="gmlp",
    )(h, pre_g, post_g, w_in, b_in, ln_g, ln_b, w_s, b_s_t, w_out,
      *[job.stack for job in jobs])
    return out[0], out[1:]


KV_DUP_WIDTH = 2 * KV_WIDTH
QKV_COLS = 512
Q_SUB = 64
SUB_KEYS = Q_SUB + WINDOW
SUM_ROWS = 16
LOG2_E = 1.4426950408889634
SCORE_LOOKAHEAD = 4


def _rope(x, cos_tab, sin_tab, is_low):
    partner = jnp.where(is_low,
                        pltpu.roll(x, V7X_LANES - ROPE_DIM // 2, axis=1),
                        pltpu.roll(x, ROPE_DIM // 2, axis=1))
    return x * cos_tab + partner * sin_tab


def _attn_body(layer, mixer, jobs, sinks_ref, h_ref, pos_ref, invf_ref, pre_ref, post_ref,
               wqkv_ref, bqkv_ref, wo_ref, *refs):
    cast_src, o_ref, cast_dst, scratch = _split_refs(refs, len(jobs))
    xn_ref, q_ref, kd_ref, vt_ref, ot_ref, cos_ref, sin_ref = scratch
    _cast_bands(pl.program_id(0) * pl.num_programs(1) + pl.program_id(1),
                jobs, cast_src, cast_dst)
    tm = h_ref.shape[0]
    n_slabs = tm // ROW_SLAB
    seq_tile = pl.program_id(1)

    @pl.when(seq_tile == 0)
    def _():
        kd_ref[0:ATTN_BLOCK, :] = jnp.zeros((ATTN_BLOCK, KV_DUP_WIDTH), BF16)
        vt_ref[:, 0:ATTN_BLOCK] = jnp.zeros((KV_WIDTH, ATTN_BLOCK), BF16)

    lane = lax.broadcasted_iota(jnp.int32, (ROW_SLAB, V7X_LANES), 1)
    first_head = lane < HEAD_DIM
    is_low = (lane % HEAD_DIM) < (ROPE_DIM // 2)

    def slab_rows(s):
        return slice(s * ROW_SLAB, (s + 1) * ROW_SLAB)

    def kv_rows(s):
        return slice(ATTN_BLOCK + s * ROW_SLAB, ATTN_BLOCK + (s + 1) * ROW_SLAB)

    def norm_task(s):
        xn_ref[slab_rows(s), :] = _rmsnorm(h_ref[slab_rows(s), :],
                                           pre_ref[layer:layer + 1, :]).astype(BF16)
        ang = pos_ref[:, slab_rows(s)].astype(F32) * invf_ref[...]
        cos, sin = jnp.cos(ang), jnp.sin(ang)
        rest = (HEAD_DIM - ROPE_DIM, ROW_SLAB)
        cos_head = [cos, cos, jnp.ones(rest, F32)]
        sin_head = [-sin, sin, jnp.zeros(rest, F32)]
        reps = V7X_LANES // HEAD_DIM
        cos_ref[slab_rows(s), :] = jnp.concatenate(cos_head * reps, axis=0).T
        sin_ref[slab_rows(s), :] = jnp.concatenate(sin_head * reps, axis=0).T

    def q_task(s, c):
        cos_q = cos_ref[slab_rows(s), :] * (HEAD_DIM ** -0.5 * LOG2_E)
        sin_q = sin_ref[slab_rows(s), :] * (HEAD_DIM ** -0.5 * LOG2_E)
        lo = c * QKV_COLS
        qc = jnp.dot(xn_ref[slab_rows(s), :], wqkv_ref[:, lo:lo + QKV_COLS],
                     preferred_element_type=F32) + bqkv_ref[mixer:mixer + 1, lo:lo + QKV_COLS]
        for j in range(QKV_COLS // V7X_LANES):
            col = qc[:, j * V7X_LANES:(j + 1) * V7X_LANES]
            q_ref[slab_rows(s), lo + j * V7X_LANES:lo + (j + 1) * V7X_LANES] = (
                _rope(col, cos_q, sin_q, is_low).astype(BF16))

    def kv_task(s):
        cos_k = cos_ref[slab_rows(s), :]
        sin_k = sin_ref[slab_rows(s), :]
        kvc = (jnp.dot(xn_ref[slab_rows(s), :], wqkv_ref[:, Q_WIDTH:Q_WIDTH + 2 * KV_WIDTH],
                       preferred_element_type=F32)
               + bqkv_ref[mixer:mixer + 1, Q_WIDTH:Q_WIDTH + 2 * KV_WIDTH])
        for j in range(KV_WIDTH // V7X_LANES):
            kj = _rope(kvc[:, j * V7X_LANES:(j + 1) * V7X_LANES], cos_k, sin_k, is_low)
            swapped = pltpu.roll(kj, HEAD_DIM, axis=1)
            even = jnp.where(first_head, kj, swapped)
            odd = jnp.where(first_head, swapped, kj)
            kd_ref[kv_rows(s), (2 * j) * V7X_LANES:(2 * j + 1) * V7X_LANES] = even.astype(BF16)
            kd_ref[kv_rows(s), (2 * j + 1) * V7X_LANES:(2 * j + 2) * V7X_LANES] = odd.astype(BF16)
        vt_ref[:, kv_rows(s)] = kvc[:, KV_WIDTH:2 * KV_WIDTH].T.astype(BF16)

    def out_task(s):
        att = ot_ref[:, slab_rows(s)].T.astype(BF16)
        mix = jnp.dot(att, wo_ref[...], preferred_element_type=F32)
        o_ref[slab_rows(s), :] = h_ref[slab_rows(s), :] + _rmsnorm(mix, post_ref[layer:layer + 1, :])

    cols4 = GQA_GROUP * Q_SUB
    sj = lax.broadcasted_iota(jnp.int32, (SUB_KEYS, cols4), 0)
    qi = lax.broadcasted_iota(jnp.int32, (SUB_KEYS, cols4), 1) % Q_SUB
    diff = ATTN_BLOCK + qi - sj
    band = (diff >= 0) & (diff < WINDOW)
    no_prev = seq_tile == 0
    band_first = [band & (sj >= jnp.where(no_prev, ATTN_BLOCK - u * Q_SUB, 0))
                  for u in range(ATTN_BLOCK // Q_SUB)]
    half = lax.broadcasted_iota(jnp.int32, (Q_SUB, V7X_LANES), 1) < HEAD_DIM
    zero_q = jnp.zeros((Q_SUB, V7X_LANES), BF16)
    head_of_lane = lax.broadcasted_iota(jnp.int32, (1, cols4), 1) // Q_SUB
    ones_rows = jnp.ones((SUM_ROWS, 2 * ATTN_BLOCK), BF16)
    zero_keys = jnp.zeros((2 * ATTN_BLOCK - SUB_KEYS, cols4), BF16)

    sinks = []
    for kv in range(N_KV_HEADS):
        sink = jnp.zeros((1, cols4), F32)
        for e in range(GQA_GROUP):
            sink = jnp.where(head_of_lane == e, sinks_ref[mixer, GQA_GROUP * kv + e], sink)
        sinks.append(sink * LOG2_E)

    def scores(kv, u):
        rows = slice(u * Q_SUB, (u + 1) * Q_SUB)
        keys = slice(u * Q_SUB, u * Q_SUB + SUB_KEYS)
        stacked = []
        for pair in range(GQA_GROUP // 2):
            qcols = slice((2 * kv + pair) * V7X_LANES, (2 * kv + pair + 1) * V7X_LANES)
            qp = q_ref[rows, qcols]
            stacked.append(jnp.where(half, qp, zero_q))
            stacked.append(jnp.where(half, zero_q, qp))
        q4 = jnp.concatenate(stacked, axis=0)
        return lax.dot_general(kd_ref[keys, kv * V7X_LANES:(kv + 1) * V7X_LANES], q4,
                               (((1,), (1,)), ((), ())), preferred_element_type=F32)

    def finish(kv, u, s):
        block = u // (ATTN_BLOCK // Q_SUB)
        part = u % (ATTN_BLOCK // Q_SUB)
        rows = slice(u * Q_SUB, (u + 1) * Q_SUB)
        window = slice(block * ATTN_BLOCK, (block + 2) * ATTN_BLOCK)
        valid = band_first[u] if block == 0 else band
        sink = sinks[kv]
        s = jnp.where(valid, s, NEG_INF)
        m = jnp.maximum(jnp.max(s, axis=0, keepdims=True), sink)
        p = jnp.exp2(s - m).astype(BF16)
        p_win = jnp.concatenate([p, zero_keys] if part == 0 else [zero_keys, p], axis=0)
        v_aug = jnp.concatenate(
            [vt_ref[kv * HEAD_DIM:(kv + 1) * HEAD_DIM, window], ones_rows], axis=0)
        ot = jnp.dot(v_aug, p_win, preferred_element_type=F32)
        denom = ot[HEAD_DIM:HEAD_DIM + 1, :] + jnp.exp2(sink - m)
        ot = ot[0:HEAD_DIM, :] * (1.0 / denom)
        for e in range(GQA_GROUP):
            head = GQA_GROUP * kv + e
            ot_ref[head * HEAD_DIM:(head + 1) * HEAD_DIM, rows] = (
                ot[:, e * Q_SUB:(e + 1) * Q_SUB])

    subs = ROW_SLAB // Q_SUB
    order = []
    for s in range(n_slabs):
        units = [(kv, u) for kv in range(N_KV_HEADS) for u in range(s * subs, (s + 1) * subs)]
        tasks = []
        if s + 1 < n_slabs:
            tasks += [functools.partial(norm_task, s + 1)]
            tasks += [functools.partial(q_task, s + 1, c) for c in range(Q_WIDTH // QKV_COLS)]
            tasks += [functools.partial(kv_task, s + 1)]
        if s > 0:
            tasks += [functools.partial(out_task, s - 1)]
        span = len(units) - SCORE_LOOKAHEAD
        after = {}
        for t, task in enumerate(tasks):
            after.setdefault(max((t + 1) * span // (len(tasks) + 1) - 1, 0), []).append(task)
        if s == 0:
            late_q = [functools.partial(q_task, 0, c) for c in range(1, Q_WIDTH // QKV_COLS)]
            after[0] = late_q + after.get(0, [])
        for i, unit in enumerate(units):
            order.append(unit)
            order.extend(after.get(i, []))
    unit_list = [e for e in order if isinstance(e, tuple)]

    norm_task(0)
    kv_task(0)
    q_task(0, 0)
    scored = []
    done = 0
    for entry in order:
        if not isinstance(entry, tuple):
            entry()
            continue
        ahead = min(done + 1 + SCORE_LOOKAHEAD, len(unit_list))
        while done + len(scored) < ahead:
            scored.append(scores(*unit_list[done + len(scored)]))
        finish(*entry, scored.pop(0))
        done += 1
    out_task(n_slabs - 1)

    kd_ref[0:ATTN_BLOCK, :] = kd_ref[tm:tm + ATTN_BLOCK, :]
    vt_ref[:, 0:ATTN_BLOCK] = vt_ref[:, tm:tm + ATTN_BLOCK]


def _attn_layer(layer, mixer, h, positions, inv_freq, pre_g, post_g, w_qkv, b_qkv, sinks, w_o,
                batch, seq, casts):
    tokens = h.shape[0]
    tm = ATTN_TILE
    tiles = seq // tm
    step_of = lambda b, i: b * tiles + i
    jobs = [_CastJob(stack, idx, batch * tiles) for stack, idx in casts]
    buffers = (4 * tm * D_MODEL * 4 + 2 * tm * V7X_LANES * 4
               + w_qkv.size * 2 + w_o.size * 2 + 2 * tm * D_MODEL * 2
               + (tm + ATTN_BLOCK) * (KV_DUP_WIDTH + KV_WIDTH) * 2
               + tm * Q_WIDTH * 4 + sum(job.vmem_bytes() for job in jobs))
    tile_map = lambda b, i: (b * tiles + i, 0)
    out = pl.pallas_call(
        functools.partial(_attn_body, layer, mixer, jobs),
        out_shape=[jax.ShapeDtypeStruct((tokens, D_MODEL), F32)]
        + [job.out_shape() for job in jobs],
        grid=(batch, tiles),
        in_specs=[
            pl.BlockSpec(memory_space=pltpu.SMEM),
            pl.BlockSpec((tm, D_MODEL), tile_map),
            pl.BlockSpec((None, 1, tm), lambda b, i: (b, 0, i)),
            _resident(inv_freq.shape),
            _resident(pre_g.shape),
            _resident(post_g.shape),
            _resident(w_qkv.shape),
            _resident(b_qkv.shape),
            _resident(w_o.shape),
        ] + [job.in_spec(step_of) for job in jobs],
        out_specs=[pl.BlockSpec((tm, D_MODEL), tile_map)]
        + [job.out_spec(step_of) for job in jobs],
        scratch_shapes=[
            pltpu.VMEM((tm, D_MODEL), BF16),
            pltpu.VMEM((tm, Q_WIDTH), BF16),
            pltpu.VMEM((tm + ATTN_BLOCK, KV_DUP_WIDTH), BF16),
            pltpu.VMEM((KV_WIDTH, tm + ATTN_BLOCK), BF16),
            pltpu.VMEM((Q_WIDTH, tm), F32),
            pltpu.VMEM((tm, V7X_LANES), F32),
            pltpu.VMEM((tm, V7X_LANES), F32),
        ],
        compiler_params=pltpu.CompilerParams(
            dimension_semantics=("arbitrary", "arbitrary"),
            vmem_limit_bytes=_vmem_limit(buffers)),
        name="swa",
    )(sinks, h, positions.reshape(batch, 1, seq), inv_freq, pre_g, post_g, w_qkv, b_qkv, w_o,
      *[job.stack for job in jobs])
    return out[0], out[1:]


def kernel(x, positions, pre_mix_g, post_mix_g, pre_ffn_g, post_ffn_g,
           a_w_in, a_b_in, a_ln_g, a_ln_b, a_w_s, a_b_s, a_w_out,
           b_w_qkv, b_b_qkv, b_sinks, b_w_o, ffn_w_gu, ffn_w_down):
    batch, seq, d_model = x.shape
    depth = pre_mix_g.shape[0]
    assert d_model == D_MODEL
    assert seq % ATTN_TILE == 0 and seq % GMLP_TILE == 0 and (batch * seq) % FFN_TILE == 0
    inv_freq = (ROPE_THETA ** (-jnp.arange(0, ROPE_DIM, 2, dtype=F32) / ROPE_DIM)).reshape(-1, 1)
    a_b_s_t = jnp.swapaxes(a_b_s, 1, 2)

    calls = []
    for i in range(depth):
        j = i // 2
        if i % 2 == 0:
            calls.append(("gmlp", i, j, [(a_w_in, j), (a_w_out, j)]))
        else:
            calls.append(("swa", i, j, [(b_w_qkv, j), (b_w_o, j)]))
        calls.append(("ffn", i, i, [(ffn_w_gu, i), (ffn_w_down, i)]))

    ready = {k: [stack[idx].astype(BF16) for stack, idx in calls[k][3]]
             for k in range(min(CAST_AHEAD, len(calls)))}
    h = x.reshape(batch * seq, d_model)
    for k, (kind, i, j, _) in enumerate(calls):
        casts = calls[k + CAST_AHEAD][3] if k + CAST_AHEAD < len(calls) else []
        w_a, w_b = ready.pop(k)
        if kind == "gmlp":
            h, cast = _gmlp_layer(i, j, h, pre_mix_g, post_mix_g, w_a, a_b_in, a_ln_g, a_ln_b,
                                  a_w_s, a_b_s_t, w_b, casts)
        elif kind == "swa":
            h, cast = _attn_layer(i, j, h, positions, inv_freq, pre_mix_g, post_mix_g, w_a,
                                  b_b_qkv, b_sinks, w_b, batch, seq, casts)
        else:
            h, cast = _ffn_layer(i, h, pre_ffn_g, post_ffn_g, w_a, w_b, casts)
        if casts:
            ready[k + CAST_AHEAD] = cast
    return h.reshape(batch, seq, d_model)
```

```python
import functools

import jax
import jax.numpy as jnp
from jax import lax
from jax.experimental import pallas as pl
from jax.experimental.pallas import tpu as pltpu

D_MODEL = 1024
CHUNK = 128
GMLP_WIDTH = 2 * D_MODEL
GMLP_GROUPS = 8
GMLP_GROUP_DIM = GMLP_WIDTH // GMLP_GROUPS
HEAD_DIM = 64
N_Q_HEADS = D_MODEL // HEAD_DIM
N_KV_HEADS = 4
GQA_GROUP = N_Q_HEADS // N_KV_HEADS
WINDOW = 128
ATTN_BLOCK = 128
ROPE_DIM = HEAD_DIM // 4
ROPE_THETA = 500000.0
Q_WIDTH = N_Q_HEADS * HEAD_DIM
KV_WIDTH = N_KV_HEADS * HEAD_DIM
FFN_HIDDEN = -(-(8 * D_MODEL) // (3 * 256)) * 256
RMS_EPS = 1e-6
LN_EPS = 1e-5
NEG_INF = -1e30

V7X_LANES = 128
V7X_VMEM_BYTES = 64 * 1024 * 1024
VMEM_HEADROOM_BYTES = 4 * 1024 * 1024

F32 = jnp.float32
BF16 = jnp.bfloat16

FFN_TILE = 1024
GMLP_TILE = 1024
ATTN_TILE = 2048
ROW_SLAB = 256
RING_SLABS = 2
GMLP_IN_COLS = 512
FFN_HIDDEN_COLS = 256


def _vmem_limit(buffer_bytes):
    return int(min(2 * buffer_bytes, V7X_VMEM_BYTES - VMEM_HEADROOM_BYTES))


def _rmsnorm(x, g):
    ms = jnp.mean(x * x, axis=-1, keepdims=True)
    return x * lax.rsqrt(ms + RMS_EPS) * g


def _resident(shape):
    zeros = (0,) * len(shape)
    return pl.BlockSpec(shape, lambda *_: zeros)


def _layer_resident(stack_shape, layer):
    index = (layer,) + (0,) * (len(stack_shape) - 1)
    return pl.BlockSpec((None,) + tuple(stack_shape[1:]), lambda *_: index,
                        pipeline_mode=pl.Buffered(1))


BF16_SUBLANES = 16
CAST_AHEAD = 1


class _CastJob:
    def __init__(self, stack, layer, steps):
        self.stack, self.layer = stack, layer
        _, self.rows, self.cols = stack.shape
        self.bands = steps
        while (self.rows % self.bands or (self.rows // self.bands) % BF16_SUBLANES):
            self.bands //= 2
        self.band_rows = self.rows // self.bands

    def in_spec(self, step_of):
        layer, last = self.layer, self.bands - 1
        return pl.BlockSpec((None, self.band_rows, self.cols),
                            lambda *ids: (layer, jnp.minimum(step_of(*ids), last), 0))

    def out_spec(self, step_of):
        last = self.bands - 1
        return pl.BlockSpec((self.band_rows, self.cols),
                            lambda *ids: (jnp.minimum(step_of(*ids), last), 0))

    def out_shape(self):
        return jax.ShapeDtypeStruct((self.rows, self.cols), BF16)

    def vmem_bytes(self):
        return 2 * self.band_rows * self.cols * (4 + 2)


def _cast_bands(step, jobs, src_refs, dst_refs):
    for job, src_ref, dst_ref in zip(jobs, src_refs, dst_refs):
        @pl.when(step < job.bands)
        def _(src_ref=src_ref, dst_ref=dst_ref):
            dst_ref[...] = src_ref[...].astype(BF16)


def _split_refs(refs, n_cast):
    return (refs[:n_cast], refs[n_cast], refs[n_cast + 1:2 * n_cast + 1],
            refs[2 * n_cast + 1:])


def _ffn_body(layer, jobs, h_ref, pre_ref, post_ref, wgu_ref, wd_ref, *refs):
    cast_src, o_ref, cast_dst, (xn_ref, act_ref) = _split_refs(refs, len(jobs))
    _cast_bands(pl.program_id(0), jobs, cast_src, cast_dst)
    n_slabs = h_ref.shape[0] // ROW_SLAB

    def rows(s):
        return slice(s * ROW_SLAB, (s + 1) * ROW_SLAB)

    def prologue(s):
        xn_ref[rows(s), :] = _rmsnorm(h_ref[rows(s), :], pre_ref[layer:layer + 1, :]).astype(BF16)

    def gate_up(s):
        xn = xn_ref[rows(s), :]
        for c in range(FFN_HIDDEN // FFN_HIDDEN_COLS):
            lo = c * FFN_HIDDEN_COLS
            gate = jnp.dot(xn, wgu_ref[:, lo:lo + FFN_HIDDEN_COLS],
                           preferred_element_type=F32)
            up = jnp.dot(xn, wgu_ref[:, FFN_HIDDEN + lo:FFN_HIDDEN + lo + FFN_HIDDEN_COLS],
                         preferred_element_type=F32)
            act = gate * (1.0 / (1.0 + jnp.exp(-gate))) * up
            act_ref[rows(s), lo:lo + FFN_HIDDEN_COLS] = act.astype(BF16)

    def down(s):
        f = jnp.dot(act_ref[rows(s), :], wd_ref[...], preferred_element_type=F32)
        o_ref[rows(s), :] = h_ref[rows(s), :] + _rmsnorm(f, post_ref[layer:layer + 1, :])

    prologue(0)
    for s in range(n_slabs):
        if s + 1 < n_slabs:
            prologue(s + 1)
        gate_up(s)
        if s > 0:
            down(s - 1)
    down(n_slabs - 1)


def _ffn_layer(layer, h, pre_g, post_g, w_gu, w_down, casts):
    tokens = h.shape[0]
    tm = FFN_TILE
    steps = tokens // tm
    step_of = lambda i: i
    jobs = [_CastJob(stack, idx, steps) for stack, idx in casts]
    buffers = (4 * tm * D_MODEL * 4 + w_gu.size * 2 + w_down.size * 2
               + tm * D_MODEL * 2 + tm * FFN_HIDDEN * 2
               + sum(job.vmem_bytes() for job in jobs))
    out = pl.pallas_call(
        functools.partial(_ffn_body, layer, jobs),
        out_shape=[jax.ShapeDtypeStruct((tokens, D_MODEL), F32)]
        + [job.out_shape() for job in jobs],
        grid=(steps,),
        in_specs=[
            pl.BlockSpec((tm, D_MODEL), lambda i: (i, 0)),
            _resident(pre_g.shape),
            _resident(post_g.shape),
            _resident(w_gu.shape),
            _resident(w_down.shape),
        ] + [job.in_spec(step_of) for job in jobs],
        out_specs=[pl.BlockSpec((tm, D_MODEL), lambda i: (i, 0))]
        + [job.out_spec(step_of) for job in jobs],
        scratch_shapes=[
            pltpu.VMEM((tm, D_MODEL), BF16),
            pltpu.VMEM((tm, FFN_HIDDEN), BF16),
        ],
        compiler_params=pltpu.CompilerParams(
            dimension_semantics=("arbitrary",),
            vmem_limit_bytes=_vmem_limit(buffers)),
        name="ffn",
    )(h, pre_g, post_g, w_gu, w_down, *[job.stack for job in jobs])
    return out[0], out[1:]


def _gmlp_body(layer, mixer, jobs, h_ref, pre_ref, post_ref, win_ref, bin_ref, lng_ref,
               lnb_ref, ws_ref, bst_ref, wout_ref, *refs):
    cast_src, o_ref, cast_dst, (xn_ref, u_ref, v_ref, gated_ref) = _split_refs(refs, len(jobs))
    _cast_bands(pl.program_id(0), jobs, cast_src, cast_dst)
    n_slabs = h_ref.shape[0] // ROW_SLAB

    def rows(s):
        return slice(s * ROW_SLAB, (s + 1) * ROW_SLAB)

    def ring(s):
        return rows(s % RING_SLABS)

    def prologue(s):
        xn_ref[rows(s), :] = _rmsnorm(h_ref[rows(s), :], pre_ref[layer:layer + 1, :]).astype(BF16)

    def project_in(s):
        xn = xn_ref[rows(s), :]
        for c in range(2 * GMLP_WIDTH // GMLP_IN_COLS):
            lo = c * GMLP_IN_COLS
            z = jnp.dot(xn, win_ref[:, lo:lo + GMLP_IN_COLS],
                        preferred_element_type=F32) + bin_ref[mixer:mixer + 1, lo:lo + GMLP_IN_COLS]
            z2 = z * (1.0 + lax.erf(z * 0.7071067811865476))
            if lo < GMLP_WIDTH:
                u_ref[ring(s), lo:lo + GMLP_IN_COLS] = z2
            else:
                v_ref[ring(s), lo - GMLP_WIDTH:lo - GMLP_WIDTH + GMLP_IN_COLS] = z2

    row = lax.broadcasted_iota(jnp.int32, (CHUNK, CHUNK), 0)
    col = lax.broadcasted_iota(jnp.int32, (CHUNK, CHUNK), 1)
    causal = col <= row
    half_w_spatial = [jnp.where(causal, 0.5 * ws_ref[g], 0.0).astype(BF16)
                      for g in range(GMLP_GROUPS)]
    half_b_spatial = 0.5 * bst_ref[...]

    def spatial_gate(s):
        base = (s % RING_SLABS) * ROW_SLAB
        for r in range(ROW_SLAB // CHUNK):
            chunk = slice(base + r * CHUNK, base + (r + 1) * CHUNK)
            v2 = v_ref[chunk, :]
            mu = jnp.mean(v2, axis=-1, keepdims=True)
            d = v2 - mu
            var = jnp.mean(d * d, axis=-1, keepdims=True)
            vn = (d * lax.rsqrt(var + 4.0 * LN_EPS) * lng_ref[mixer:mixer + 1, :]
                  + lnb_ref[mixer:mixer + 1, :]).astype(BF16)
            for g in range(GMLP_GROUPS):
                cols = slice(g * GMLP_GROUP_DIM, (g + 1) * GMLP_GROUP_DIM)
                half_sv = jnp.dot(half_w_spatial[g], vn[:, cols], preferred_element_type=F32)
                half_sv = half_sv + half_b_spatial[:, g:g + 1]
                gated_ref[chunk, cols] = (u_ref[chunk, cols] * half_sv).astype(BF16)

    def project_out(s):
        mix = jnp.dot(gated_ref[ring(s), :], wout_ref[...], preferred_element_type=F32)
        o_ref[rows(s), :] = h_ref[rows(s), :] + _rmsnorm(mix, post_ref[layer:layer + 1, :])

    prologue(0)
    project_in(0)
    for s in range(n_slabs):
        if s + 1 < n_slabs:
            prologue(s + 1)
            project_in(s + 1)
        spatial_gate(s)
        project_out(s)


def _gmlp_layer(layer, mixer, h, pre_g, post_g, w_in, b_in, ln_g, ln_b, w_s, b_s_t, w_out,
                casts):
    tokens = h.shape[0]
    tm = GMLP_TILE
    steps = tokens // tm
    step_of = lambda i: i
    jobs = [_CastJob(stack, idx, steps) for stack, idx in casts]
    ring_rows = RING_SLABS * ROW_SLAB
    buffers = (4 * tm * D_MODEL * 4 + w_in.size * 2 + w_out.size * 2
               + tm * D_MODEL * 2 + 2 * ring_rows * GMLP_WIDTH * 4
               + ring_rows * GMLP_WIDTH * 2 + sum(job.vmem_bytes() for job in jobs))
    out = pl.pallas_call(
        functools.partial(_gmlp_body, layer, mixer, jobs),
        out_shape=[jax.ShapeDtypeStruct((tokens, D_MODEL), F32)]
        + [job.out_shape() for job in jobs],
        grid=(steps,),
        in_specs=[
            pl.BlockSpec((tm, D_MODEL), lambda i: (i, 0)),
            _resident(pre_g.shape),
            _resident(post_g.shape),
            _resident(w_in.shape),
            _resident(b_in.shape),
            _resident(ln_g.shape),
            _resident(ln_b.shape),
            _layer_resident(w_s.shape, mixer),
            _layer_resident(b_s_t.shape, mixer),
            _resident(w_out.shape),
        ] + [job.in_spec(step_of) for job in jobs],
        out_specs=[pl.BlockSpec((tm, D_MODEL), lambda i: (i, 0))]
        + [job.out_spec(step_of) for job in jobs],
        scratch_shapes=[
            pltpu.VMEM((tm, D_MODEL), BF16),
            pltpu.VMEM((ring_rows, GMLP_WIDTH), F32),
            pltpu.VMEM((ring_rows, GMLP_WIDTH), F32),
            pltpu.VMEM((ring_rows, GMLP_WIDTH), BF16),
        ],
        compiler_params=pltpu.CompilerParams(
            dimension_semantics=("arbitrary",),
            vmem_limit_bytes=_vmem_limit(buffers)),
        name="gmlp",
    )(h, pre_g, post_g, w_in, b_in, ln_g, ln_b, w_s, b_s_t, w_out,
      *[job.stack for job in jobs])
    return out[0], out[1:]


KV_DUP_WIDTH = 2 * KV_WIDTH
QKV_COLS = 512
Q_SUB = 64
SUB_KEYS = Q_SUB + WINDOW
SUM_ROWS = 16
LOG2_E = 1.4426950408889634
SCORE_LOOKAHEAD = 4


def _rope(x, cos_tab, sin_tab, is_low):
    partner = jnp.where(is_low,
                        pltpu.roll(x, V7X_LANES - ROPE_DIM // 2, axis=1),
                        pltpu.roll(x, ROPE_DIM // 2, axis=1))
    return x * cos_tab + partner * sin_tab


def _attn_body(layer, mixer, jobs, sinks_ref, h_ref, pos_ref, invf_ref, pre_ref, post_ref,
               wqkv_ref, bqkv_ref, wo_ref, *refs):
    cast_src, o_ref, cast_dst, scratch = _split_refs(refs, len(jobs))
    xn_ref, q_ref, kd_ref, vt_ref, ot_ref, cos_ref, sin_ref = scratch
    _cast_bands(pl.program_id(0) * pl.num_programs(1) + pl.program_id(1),
                jobs, cast_src, cast_dst)
    tm = h_ref.shape[0]
    n_slabs = tm // ROW_SLAB
    seq_tile = pl.program_id(1)

    @pl.when(seq_tile == 0)
    def _():
        kd_ref[0:ATTN_BLOCK, :] = jnp.zeros((ATTN_BLOCK, KV_DUP_WIDTH), BF16)
        vt_ref[:, 0:ATTN_BLOCK] = jnp.zeros((KV_WIDTH, ATTN_BLOCK), BF16)

    lane = lax.broadcasted_iota(jnp.int32, (ROW_SLAB, V7X_LANES), 1)
    first_head = lane < HEAD_DIM
    is_low = (lane % HEAD_DIM) < (ROPE_DIM // 2)

    def slab_rows(s):
        return slice(s * ROW_SLAB, (s + 1) * ROW_SLAB)

    def kv_rows(s):
        return slice(ATTN_BLOCK + s * ROW_SLAB, ATTN_BLOCK + (s + 1) * ROW_SLAB)

    def ring_rows(s):
        return slab_rows(s % RING_SLABS)

    def ring_sub(u):
        base = (u * Q_SUB // ROW_SLAB % RING_SLABS) * ROW_SLAB + u * Q_SUB % ROW_SLAB
        return slice(base, base + Q_SUB)

    def norm_task(s):
        xn_ref[ring_rows(s), :] = _rmsnorm(h_ref[slab_rows(s), :],
                                           pre_ref[layer:layer + 1, :]).astype(BF16)
        ang = pos_ref[:, slab_rows(s)].astype(F32) * invf_ref[...]
        cos, sin = jnp.cos(ang), jnp.sin(ang)
        rest = (HEAD_DIM - ROPE_DIM, ROW_SLAB)
        cos_head = [cos, cos, jnp.ones(rest, F32)]
        sin_head = [-sin, sin, jnp.zeros(rest, F32)]
        reps = V7X_LANES // HEAD_DIM
        cos_ref[ring_rows(s), :] = jnp.concatenate(cos_head * reps, axis=0).T
        sin_ref[ring_rows(s), :] = jnp.concatenate(sin_head * reps, axis=0).T

    def q_task(s, c):
        cos_q = cos_ref[ring_rows(s), :] * (HEAD_DIM ** -0.5 * LOG2_E)
        sin_q = sin_ref[ring_rows(s), :] * (HEAD_DIM ** -0.5 * LOG2_E)
        lo = c * QKV_COLS
        qc = jnp.dot(xn_ref[ring_rows(s), :], wqkv_ref[:, lo:lo + QKV_COLS],
                     preferred_element_type=F32) + bqkv_ref[mixer:mixer + 1, lo:lo + QKV_COLS]
        for j in range(QKV_COLS // V7X_LANES):
            col = qc[:, j * V7X_LANES:(j + 1) * V7X_LANES]
            q_ref[ring_rows(s), lo + j * V7X_LANES:lo + (j + 1) * V7X_LANES] = (
                _rope(col, cos_q, sin_q, is_low).astype(BF16))

    def kv_task(s):
        cos_k = cos_ref[ring_rows(s), :]
        sin_k = sin_ref[ring_rows(s), :]
        kvc = (jnp.dot(xn_ref[ring_rows(s), :], wqkv_ref[:, Q_WIDTH:Q_WIDTH + 2 * KV_WIDTH],
                       preferred_element_type=F32)
               + bqkv_ref[mixer:mixer + 1, Q_WIDTH:Q_WIDTH + 2 * KV_WIDTH])
        for j in range(KV_WIDTH // V7X_LANES):
            kj = _rope(kvc[:, j * V7X_LANES:(j + 1) * V7X_LANES], cos_k, sin_k, is_low)
            swapped = pltpu.roll(kj, HEAD_DIM, axis=1)
            even = jnp.where(first_head, kj, swapped)
            odd = jnp.where(first_head, swapped, kj)
            kd_ref[kv_rows(s), (2 * j) * V7X_LANES:(2 * j + 1) * V7X_LANES] = even.astype(BF16)
            kd_ref[kv_rows(s), (2 * j + 1) * V7X_LANES:(2 * j + 2) * V7X_LANES] = odd.astype(BF16)
        vt_ref[:, kv_rows(s)] = kvc[:, KV_WIDTH:2 * KV_WIDTH].T.astype(BF16)

    def out_task(s):
        att = ot_ref[:, ring_rows(s)].T.astype(BF16)
        mix = jnp.dot(att, wo_ref[...], preferred_element_type=F32)
        o_ref[slab_rows(s), :] = h_ref[slab_rows(s), :] + _rmsnorm(mix, post_ref[layer:layer + 1, :])

    cols4 = GQA_GROUP * Q_SUB
    sj = lax.broadcasted_iota(jnp.int32, (SUB_KEYS, cols4), 0)
    qi = lax.broadcasted_iota(jnp.int32, (SUB_KEYS, cols4), 1) % Q_SUB
    diff = ATTN_BLOCK + qi - sj
    band = (diff >= 0) & (diff < WINDOW)
    no_prev = seq_tile == 0
    band_first = [band & (sj >= jnp.where(no_prev, ATTN_BLOCK - u * Q_SUB, 0))
                  for u in range(ATTN_BLOCK // Q_SUB)]
    half = lax.broadcasted_iota(jnp.int32, (Q_SUB, V7X_LANES), 1) < HEAD_DIM
    zero_q = jnp.zeros((Q_SUB, V7X_LANES), BF16)
    head_of_lane = lax.broadcasted_iota(jnp.int32, (1, cols4), 1) // Q_SUB
    ones_rows = jnp.ones((SUM_ROWS, 2 * ATTN_BLOCK), BF16)
    zero_keys = jnp.zeros((2 * ATTN_BLOCK - SUB_KEYS, cols4), BF16)

    sinks = []
    for kv in range(N_KV_HEADS):
        sink = jnp.zeros((1, cols4), F32)
        for e in range(GQA_GROUP):
            sink = jnp.where(head_of_lane == e, sinks_ref[mixer, GQA_GROUP * kv + e], sink)
        sinks.append(sink * LOG2_E)

    def scores(kv, u):
        rows = ring_sub(u)
        keys = slice(u * Q_SUB, u * Q_SUB + SUB_KEYS)
        stacked = []
        for pair in range(GQA_GROUP // 2):
            qcols = slice((2 * kv + pair) * V7X_LANES, (2 * kv + pair + 1) * V7X_LANES)
            qp = q_ref[rows, qcols]
            stacked.append(jnp.where(half, qp, zero_q))
            stacked.append(jnp.where(half, zero_q, qp))
        q4 = jnp.concatenate(stacked, axis=0)
        return lax.dot_general(kd_ref[keys, kv * V7X_LANES:(kv + 1) * V7X_LANES], q4,
                               (((1,), (1,)), ((), ())), preferred_element_type=F32)

    def finish(kv, u, s):
        block = u // (ATTN_BLOCK // Q_SUB)
        part = u % (ATTN_BLOCK // Q_SUB)
        rows = ring_sub(u)
        window = slice(block * ATTN_BLOCK, (block + 2) * ATTN_BLOCK)
        valid = band_first[u] if block == 0 else band
        sink = sinks[kv]
        s = jnp.where(valid, s, NEG_INF)
        m = jnp.maximum(jnp.max(s, axis=0, keepdims=True), sink)
        p = jnp.exp2(s - m).astype(BF16)
        p_win = jnp.concatenate([p, zero_keys] if part == 0 else [zero_keys, p], axis=0)
        v_aug = jnp.concatenate(
            [vt_ref[kv * HEAD_DIM:(kv + 1) * HEAD_DIM, window], ones_rows], axis=0)
        ot = jnp.dot(v_aug, p_win, preferred_element_type=F32)
        denom = ot[HEAD_DIM:HEAD_DIM + 1, :] + jnp.exp2(sink - m)
        ot = ot[0:HEAD_DIM, :] * (1.0 / denom)
        for e in range(GQA_GROUP):
            head = GQA_GROUP * kv + e
            ot_ref[head * HEAD_DIM:(head + 1) * HEAD_DIM, rows] = (
                ot[:, e * Q_SUB:(e + 1) * Q_SUB])

    subs = ROW_SLAB // Q_SUB
    order = []
    for s in range(n_slabs):
        units = [(kv, u) for kv in range(N_KV_HEADS) for u in range(s * subs, (s + 1) * subs)]
        tasks = []
        if s + 1 < n_slabs:
            tasks += [functools.partial(norm_task, s + 1)]
            tasks += [functools.partial(q_task, s + 1, c) for c in range(Q_WIDTH // QKV_COLS)]
            tasks += [functools.partial(kv_task, s + 1)]
        if s > 0:
            tasks += [functools.partial(out_task, s - 1)]
        span = len(units) - SCORE_LOOKAHEAD
        after = {}
        for t, task in enumerate(tasks):
            after.setdefault(max((t + 1) * span // (len(tasks) + 1) - 1, 0), []).append(task)
        for i, unit in enumerate(units):
            order.append(unit)
            order.extend(after.get(i, []))
    unit_list = [e for e in order if isinstance(e, tuple)]

    norm_task(0)
    for c in range(Q_WIDTH // QKV_COLS):
        q_task(0, c)
    kv_task(0)
    scored = []
    done = 0
    for entry in order:
        if not isinstance(entry, tuple):
            entry()
            continue
        ahead = min(done + 1 + SCORE_LOOKAHEAD, len(unit_list))
        while done + len(scored) < ahead:
            scored.append(scores(*unit_list[done + len(scored)]))
        finish(*entry, scored.pop(0))
        done += 1
    out_task(n_slabs - 1)

    kd_ref[0:ATTN_BLOCK, :] = kd_ref[tm:tm + ATTN_BLOCK, :]
    vt_ref[:, 0:ATTN_BLOCK] = vt_ref[:, tm:tm + ATTN_BLOCK]


def _attn_layer(layer, mixer, h, positions, inv_freq, pre_g, post_g, w_qkv, b_qkv, sinks, w_o,
                batch, seq, casts):
    tokens = h.shape[0]
    tm = ATTN_TILE
    tiles = seq // tm
    step_of = lambda b, i: b * tiles + i
    jobs = [_CastJob(stack, idx, batch * tiles) for stack, idx in casts]
    ring = RING_SLABS * ROW_SLAB
    buffers = (4 * tm * D_MODEL * 4 + 2 * ring * V7X_LANES * 4
               + w_qkv.size * 2 + w_o.size * 2 + 2 * ring * D_MODEL * 2
               + (tm + ATTN_BLOCK) * (KV_DUP_WIDTH + KV_WIDTH) * 2
               + ring * Q_WIDTH * 4 + sum(job.vmem_bytes() for job in jobs))
    tile_map = lambda b, i: (b * tiles + i, 0)
    out = pl.pallas_call(
        functools.partial(_attn_body, layer, mixer, jobs),
        out_shape=[jax.ShapeDtypeStruct((tokens, D_MODEL), F32)]
        + [job.out_shape() for job in jobs],
        grid=(batch, tiles),
        in_specs=[
            pl.BlockSpec(memory_space=pltpu.SMEM),
            pl.BlockSpec((tm, D_MODEL), tile_map),
            pl.BlockSpec((None, 1, tm), lambda b, i: (b, 0, i)),
            _resident(inv_freq.shape),
            _resident(pre_g.shape),
            _resident(post_g.shape),
            _resident(w_qkv.shape),
            _resident(b_qkv.shape),
            _resident(w_o.shape),
        ] + [job.in_spec(step_of) for job in jobs],
        out_specs=[pl.BlockSpec((tm, D_MODEL), tile_map)]
        + [job.out_spec(step_of) for job in jobs],
        scratch_shapes=[
            pltpu.VMEM((ring, D_MODEL), BF16),
            pltpu.VMEM((ring, Q_WIDTH), BF16),
            pltpu.VMEM((tm + ATTN_BLOCK, KV_DUP_WIDTH), BF16),
            pltpu.VMEM((KV_WIDTH, tm + ATTN_BLOCK), BF16),
            pltpu.VMEM((Q_WIDTH, ring), F32),
            pltpu.VMEM((ring, V7X_LANES), F32),
            pltpu.VMEM((ring, V7X_LANES), F32),
        ],
        compiler_params=pltpu.CompilerParams(
            dimension_semantics=("arbitrary", "arbitrary"),
            vmem_limit_bytes=_vmem_limit(buffers)),
        name="swa",
    )(sinks, h, positions.reshape(batch, 1, seq), inv_freq, pre_g, post_g, w_qkv, b_qkv, w_o,
      *[job.stack for job in jobs])
    return out[0], out[1:]


def kernel(x, positions, pre_mix_g, post_mix_g, pre_ffn_g, post_ffn_g,
           a_w_in, a_b_in, a_ln_g, a_ln_b, a_w_s, a_b_s, a_w_out,
           b_w_qkv, b_b_qkv, b_sinks, b_w_o, ffn_w_gu, ffn_w_down):
    batch, seq, d_model = x.shape
    depth = pre_mix_g.shape[0]
    assert d_model == D_MODEL
    assert seq % ATTN_TILE == 0 and seq % GMLP_TILE == 0 and (batch * seq) % FFN_TILE == 0
    inv_freq = (ROPE_THETA ** (-jnp.arange(0, ROPE_DIM, 2, dtype=F32) / ROPE_DIM)).reshape(-1, 1)
    a_b_s_t = jnp.swapaxes(a_b_s, 1, 2)

    calls = []
    for i in range(depth):
        j = i // 2
        if i % 2 == 0:
            calls.append(("gmlp", i, j, [(a_w_in, j), (a_w_out, j)]))
        else:
            calls.append(("swa", i, j, [(b_w_qkv, j), (b_w_o, j)]))
        calls.append(("ffn", i, i, [(ffn_w_gu, i), (ffn_w_down, i)]))

    ready = {k: [stack[idx].astype(BF16) for stack, idx in calls[k][3]]
             for k in range(min(CAST_AHEAD, len(calls)))}
    h = x.reshape(batch * seq, d_model)
    for k, (kind, i, j, _) in enumerate(calls):
        casts = calls[k + CAST_AHEAD][3] if k + CAST_AHEAD < len(calls) else []
        w_a, w_b = ready.pop(k)
        if kind == "gmlp":
            h, cast = _gmlp_layer(i, j, h, pre_mix_g, post_mix_g, w_a, a_b_in, a_ln_g, a_ln_b,
                                  a_w_s, a_b_s_t, w_b, casts)
        elif kind == "swa":
            h, cast = _attn_layer(i, j, h, positions, inv_freq, pre_mix_g, post_mix_g, w_a,
                                  b_b_qkv, b_sinks, w_b, batch, seq, casts)
        else:
            h, cast = _ffn_layer(i, h, pre_ffn_g, post_ffn_g, w_a, w_b, casts)
        if casts:
            ready[k + CAST_AHEAD] = cast
    return h.reshape(batch, seq, d_model)
```
